```python
import math
import jax, jax.numpy as jnp
from jax import lax
import numpy as np

D_MODEL = 1024
BATCH = 8
SEQ = 2048
DEPTH = 2
DEC_BATCH = 32
DEC_SEQ = 1
PAST_LEN = 16384
PAGE_SIZE = 128

N_MIXERS = 2
N_ATTN = (DEPTH + 1) // 2
N_CONV = DEPTH // 2
N_HEADS = 8
HEAD_DIM = 64
V_DIM = 2 * HEAD_DIM
ATTN_WIDTH = N_HEADS * V_DIM
SCALE = HEAD_DIM ** -0.5
Q_BLOCK = 128
CONV_WIDTH = 31
CONV_CH = D_MODEL
D_FF = 2816
FFN_CONV_WIDTH = 3
RMS_EPS = 1e-6
LN_EPS = 1e-5
LAMBDA_STD = 0.1

kernel_name = "hybrid_diffattn_conformer_convffn_step"


def _rmsnorm(x, g):
    xf = x.astype(jnp.float32)
    y = xf * lax.rsqrt(jnp.mean(xf * xf, axis=-1, keepdims=True) + RMS_EPS)
    return (y * g.astype(jnp.float32)).astype(x.dtype)


def _layernorm(x, g, b):
    xf = x.astype(jnp.float32)
    mu = jnp.mean(xf, axis=-1, keepdims=True)
    xc = xf - mu
    y = xc * lax.rsqrt(jnp.mean(xc * xc, axis=-1, keepdims=True) + LN_EPS)
    return (y * g.astype(jnp.float32) + b.astype(jnp.float32)).astype(x.dtype)


def _causal_dwconv(x, buf, w, b):
    C = x.shape[-1]
    xp = jnp.concatenate([buf.astype(x.dtype), x], axis=1)
    y = lax.conv_general_dilated(xp, w[:, None, :].astype(x.dtype), window_strides=(1,), padding='VALID',
                                 dimension_numbers=('NWC', 'WIO', 'NWC'), feature_group_count=C)
    return y + b, xp[:, xp.shape[1] - (w.shape[0] - 1):]


def _qkv(h, w):
    B, T, _ = h.shape
    q, k, v = jnp.split(h @ w, 3, axis=-1)
    return (q.reshape(B, T, N_HEADS, 2, HEAD_DIM), k.reshape(B, T, N_HEADS, V_DIM),
            v.reshape(B, T, N_HEADS, V_DIM))


def _diff_lambda(lq1, lk1, lq2, lk2, lam_init):
    f = jnp.float32
    return (jnp.exp(jnp.sum(lq1.astype(f) * lk1.astype(f))) - jnp.exp(jnp.sum(lq2.astype(f) * lk2.astype(f)))
            + lam_init)


def _diff_attn_prompt(q, k, v, lam):
    B, T = q.shape[:2]
    nb = T // Q_BLOCK
    k5 = k.reshape(B, T, N_HEADS, 2, HEAD_DIM)
    vf = v.astype(jnp.float32)
    qb = q.reshape(B, nb, Q_BLOCK, N_HEADS, 2, HEAD_DIM).transpose(1, 0, 2, 3, 4, 5)
    kpos = jnp.arange(T)

    def block(args):
        qi, bi = args
        s = jnp.einsum('bqhcd,bkhcd->bhcqk', qi, k5, preferred_element_type=jnp.float32) * SCALE
        qpos = bi * Q_BLOCK + jnp.arange(Q_BLOCK)
        s = jnp.where(kpos[None, :] <= qpos[:, None], s, -jnp.inf)
        p = jax.nn.softmax(s, axis=-1)
        a = p[:, :, 0] - lam * p[:, :, 1]
        return jnp.einsum('bhqk,bkhe->bqhe', a, vf)

    out = lax.map(block, (qb, jnp.arange(nb)))
    return out.transpose(1, 0, 2, 3, 4).reshape(B, T, N_HEADS, V_DIM)


def _online_update(carry, q, kr, vr, mask):
    m, l, acc = carry
    B, K = kr.shape[:2]
    k5 = kr.reshape(B, K, N_HEADS, 2, HEAD_DIM)
    s = jnp.einsum('bqhcd,bkhcd->bhcqk', q, k5, preferred_element_type=jnp.float32) * SCALE
    if mask is not None:
        s = jnp.where(mask, s, -jnp.inf)
    m_new = jnp.maximum(m, jnp.max(s, axis=-1))
    p = jnp.exp(s - m_new[..., None])
    alpha = jnp.exp(m - m_new)
    l = alpha * l + jnp.sum(p, axis=-1)
    acc = alpha[..., None] * acc + jnp.einsum('bhcqk,bkhe->bhcqe', p, vr.astype(jnp.float32))
    return (m_new, l, acc)


def _diff_attn_sample(q, k_new, v_new, cache_k, cache_v, j, page_table, lam):
    B, T = q.shape[:2]
    init = (jnp.full((B, N_HEADS, 2, T), -jnp.inf, jnp.float32),
            jnp.zeros((B, N_HEADS, 2, T), jnp.float32),
            jnp.zeros((B, N_HEADS, 2, T, V_DIM), jnp.float32))

    def page_step(carry, pages):
        return _online_update(carry, q, cache_k[j, pages], cache_v[j, pages], None), None

    carry, _ = lax.scan(page_step, init, page_table.T)
    causal = jnp.tril(jnp.ones((T, T), dtype=bool))
    m, l, acc = _online_update(carry, q, k_new, v_new, causal)
    o = acc / l[..., None]
    out = o[:, :, 0] - lam * o[:, :, 1]
    return out.transpose(0, 2, 1, 3)


def _diff_out(o, g, lam_init, w_o, dtype):
    B, T = o.shape[:2]
    o = _rmsnorm(o, g) * (1.0 - lam_init)
    return o.reshape(B, T, ATTN_WIDTH).astype(dtype) @ w_o


def _conformer_conv(h, buf, w_pw1, b_pw1, w_dw, b_dw, ln_g, ln_b, w_pw2, b_pw2):
    a = h @ w_pw1 + b_pw1
    u = a[..., :CONV_CH] * jax.nn.sigmoid(a[..., CONV_CH:])
    c, new_buf = _causal_dwconv(u, buf, w_dw, b_dw)
    c = jax.nn.silu(_layernorm(c, ln_g, ln_b))
    return c @ w_pw2 + b_pw2, new_buf


def _conv_ffn(h, buf, w_up, w_dw, b_dw, w_down):
    u = h @ w_up
    c, new_buf = _causal_dwconv(u, buf, w_dw, b_dw)
    return (jax.nn.silu(c[..., :D_FF]) * c[..., D_FF:]) @ w_down, new_buf


def setup_inputs(seed: int = 0) -> dict:
    key = jax.random.key(seed)
    ks = jax.random.split(key, 32)
    f = jnp.float32
    n_pages = PAST_LEN // PAGE_SIZE
    n_used = DEC_BATCH * n_pages
    n_pool = (5 * n_used + 3) // 4

    def nrm(k, shape, scale):
        return jax.random.normal(k, shape, f) * scale

    page_table = jax.random.permutation(ks[0], n_pool)[:n_used].reshape(DEC_BATCH, n_pages).astype(jnp.int32)
    return {
        "x_prompt": nrm(ks[1], (BATCH, SEQ, D_MODEL), 1.0),
        "x_sample": nrm(ks[2], (DEC_BATCH, DEC_SEQ, D_MODEL), 1.0),
        "cache_k": nrm(ks[3], (N_ATTN, n_pool, PAGE_SIZE, N_HEADS, V_DIM), 1.0),
        "cache_v": nrm(ks[4], (N_ATTN, n_pool, PAGE_SIZE, N_HEADS, V_DIM), 1.0),
        "page_table": page_table,
        "state_conv": nrm(ks[5], (N_CONV, DEC_BATCH, CONV_WIDTH - 1, CONV_CH), 1.0),
        "state_ffn": nrm(ks[6], (DEPTH, DEC_BATCH, FFN_CONV_WIDTH - 1, 2 * D_FF), 1.0),
        "norm_mix": 1.0 + nrm(ks[7], (DEPTH, D_MODEL), 0.02),
        "norm_ffn": 1.0 + nrm(ks[8], (DEPTH, D_MODEL), 0.02),
        "norm_final": 1.0 + nrm(ks[9], (D_MODEL,), 0.02),
        "w_qkv": nrm(ks[10], (N_ATTN, D_MODEL, 3 * ATTN_WIDTH), D_MODEL ** -0.5),
        "w_o": nrm(ks[11], (N_ATTN, ATTN_WIDTH, D_MODEL), ATTN_WIDTH ** -0.5),
        "lambda_q1": nrm(ks[12], (N_ATTN, HEAD_DIM), LAMBDA_STD),
        "lambda_k1": nrm(ks[13], (N_ATTN, HEAD_DIM), LAMBDA_STD),
        "lambda_q2": nrm(ks[14], (N_ATTN, HEAD_DIM), LAMBDA_STD),
        "lambda_k2": nrm(ks[15], (N_ATTN, HEAD_DIM), LAMBDA_STD),
        "subln_g": 1.0 + nrm(ks[16], (N_ATTN, V_DIM), 0.02),
        "w_pw1": nrm(ks[17], (N_CONV, D_MODEL, 2 * CONV_CH), D_MODEL ** -0.5),
        "b_pw1": nrm(ks[18], (N_CONV, 2 * CONV_CH), 0.02),
        "w_dw": nrm(ks[19], (N_CONV, CONV_WIDTH, CONV_CH), CONV_WIDTH ** -0.5),
        "b_dw": nrm(ks[20], (N_CONV, CONV_CH), 0.02),
        "ln_g": 1.0 + nrm(ks[21], (N_CONV, CONV_CH), 0.02),
        "ln_b": nrm(ks[22], (N_CONV, CONV_CH), 0.02),
        "w_pw2": nrm(ks[23], (N_CONV, CONV_CH, D_MODEL), CONV_CH ** -0.5),
        "b_pw2": nrm(ks[24], (N_CONV, D_MODEL), 0.02),
        "w_up": nrm(ks[25], (DEPTH, D_MODEL, 2 * D_FF), D_MODEL ** -0.5),
        "w_ffn_dw": nrm(ks[26], (DEPTH, FFN_CONV_WIDTH, 2 * D_FF), FFN_CONV_WIDTH ** -0.5),
        "b_ffn_dw": nrm(ks[27], (DEPTH, 2 * D_FF), 0.02),
        "w_down": nrm(ks[28], (DEPTH, D_FF, D_MODEL), D_FF ** -0.5),
    }


def reference(x_prompt, x_sample, cache_k, cache_v, page_table, state_conv, state_ffn,
              norm_mix, norm_ffn, norm_final, w_qkv, w_o, lambda_q1, lambda_k1, lambda_q2, lambda_k2,
              subln_g, w_pw1, b_pw1, w_dw, b_dw, ln_g, ln_b, w_pw2, b_pw2,
              w_up, w_ffn_dw, b_ffn_dw, w_down):
    xp, xs = x_prompt, x_sample
    Bp, Bs = xp.shape[0], xs.shape[0]
    kp_l, vp_l, ks_l, vs_l = [], [], [], []
    cp_l, cs_l, fp_l, fs_l = [], [], [], []
    for i in range(DEPTH):
        j = i // N_MIXERS
        hp = _rmsnorm(xp, norm_mix[i])
        hs = _rmsnorm(xs, norm_mix[i])
        if i % N_MIXERS == 0:
            lam_init = 0.8 - 0.6 * math.exp(-0.3 * i)
            lam = _diff_lambda(lambda_q1[j], lambda_k1[j], lambda_q2[j], lambda_k2[j], lam_init)
            qp, kp, vp = _qkv(hp, w_qkv[j])
            qs, kss, vss = _qkv(hs, w_qkv[j])
            op = _diff_attn_prompt(qp, kp, vp, lam)
            os_ = _diff_attn_sample(qs, kss, vss, cache_k, cache_v, j, page_table, lam)
            xp = xp + _diff_out(op, subln_g[j], lam_init, w_o[j], xp.dtype)
            xs = xs + _diff_out(os_, subln_g[j], lam_init, w_o[j], xs.dtype)
            kp_l.append(kp); vp_l.append(vp); ks_l.append(kss); vs_l.append(vss)
        else:
            zero_buf = jnp.zeros((Bp, CONV_WIDTH - 1, CONV_CH), xp.dtype)
            yp, bp = _conformer_conv(hp, zero_buf, w_pw1[j], b_pw1[j], w_dw[j], b_dw[j],
                                     ln_g[j], ln_b[j], w_pw2[j], b_pw2[j])
            ys, bs = _conformer_conv(hs, state_conv[j], w_pw1[j], b_pw1[j], w_dw[j], b_dw[j],
                                     ln_g[j], ln_b[j], w_pw2[j], b_pw2[j])
            xp = xp + yp
            xs = xs + ys
            cp_l.append(bp); cs_l.append(bs)
        hp = _rmsnorm(xp, norm_ffn[i])
        hs = _rmsnorm(xs, norm_ffn[i])
        zero_fbuf = jnp.zeros((Bp, FFN_CONV_WIDTH - 1, 2 * D_FF), xp.dtype)
        yp, fbp = _conv_ffn(hp, zero_fbuf, w_up[i], w_ffn_dw[i], b_ffn_dw[i], w_down[i])
        ys, fbs = _conv_ffn(hs, state_ffn[i], w_up[i], w_ffn_dw[i], b_ffn_dw[i], w_down[i])
        xp = xp + yp
        xs = xs + ys
        fp_l.append(fbp); fs_l.append(fbs)
    y_prompt = _rmsnorm(xp, norm_final)
    y_sample = _rmsnorm(xs, norm_final)
    new_k_prompt = jnp.stack(kp_l)
    new_v_prompt = jnp.stack(vp_l)
    new_k_sample = jnp.stack(ks_l)
    new_v_sample = jnp.stack(vs_l)
    new_conv_prompt = jnp.stack(cp_l)
    new_conv_sample = jnp.stack(cs_l)
    new_ffn_prompt = jnp.stack(fp_l)
    new_ffn_sample = jnp.stack(fs_l)
    return (y_prompt, y_sample, new_k_prompt, new_v_prompt, new_k_sample, new_v_sample,
            new_conv_prompt, new_conv_sample, new_ffn_prompt, new_ffn_sample)
```

```python
import functools
import math

import jax
import jax.numpy as jnp
from jax import lax
from jax.experimental import pallas as pl
from jax.experimental.pallas import tpu as pltpu

F32 = jnp.float32
BF16 = jnp.bfloat16

RMS_EPS = 1e-6
LN_EPS = 1e-5
N_MIXERS = 2
LANES = 128
SUBLANES = 8
VMEM_LIMIT = 56 * 1024 * 1024


def _params(*sem):
    return pltpu.CompilerParams(dimension_semantics=sem, vmem_limit_bytes=VMEM_LIMIT)


def _rms_rows(x, g, eps):
    ms = jnp.mean(x * x, axis=-1, keepdims=True)
    return x * lax.rsqrt(ms + eps) * g


def _diff_lambda(lq1, lk1, lq2, lk2, lam_init):
    a = jnp.sum(lq1[...] * lk1[...], keepdims=True)
    b = jnp.sum(lq2[...] * lk2[...], keepdims=True)
    return jnp.exp(a) - jnp.exp(b) + lam_init


def _rmsnorm_kernel(x_ref, g_ref, o_ref):
    o_ref[...] = _rms_rows(x_ref[...], g_ref[...], RMS_EPS).astype(o_ref.dtype)


def _rmsnorm(x, g, tm, out_dtype):
    M, D = x.shape
    return pl.pallas_call(
        _rmsnorm_kernel,
        name="rmsnorm",
        grid=(M // tm,),
        in_specs=[pl.BlockSpec((tm, D), lambda m: (m, 0)),
                  pl.BlockSpec((1, D), lambda m: (0, 0))],
        out_specs=pl.BlockSpec((tm, D), lambda m: (m, 0)),
        out_shape=jax.ShapeDtypeStruct((M, D), out_dtype),
        compiler_params=_params("parallel"),
    )(x, g.reshape(1, D))


def _qkv_kernel(h_ref, w_ref, q_ref, k_ref, v_ref, *, width, scale):
    h = h_ref[...]
    q = jnp.dot(h, w_ref[:, 0:width], preferred_element_type=F32)
    q_ref[...] = (q * scale).astype(q_ref.dtype)
    k_ref[...] = jnp.dot(h, w_ref[:, width:2 * width], preferred_element_type=F32)
    v_ref[...] = jnp.dot(h, w_ref[:, 2 * width:3 * width], preferred_element_type=F32)


def _qkv(h, w, tm, scale, q_dtype):
    M, D = h.shape
    width = w.shape[1] // 3
    blk = lambda m: (m, 0)
    return pl.pallas_call(
        functools.partial(_qkv_kernel, width=width, scale=scale),
        name="qkv",
        grid=(M // tm,),
        in_specs=[pl.BlockSpec((tm, D), blk),
                  pl.BlockSpec(w.shape, lambda m: (0, 0))],
        out_specs=[pl.BlockSpec((tm, width), blk)] * 3,
        out_shape=[jax.ShapeDtypeStruct((M, width), q_dtype),
                   jax.ShapeDtypeStruct((M, width), F32),
                   jax.ShapeDtypeStruct((M, width), F32)],
        compiler_params=_params("parallel"),
    )(h, w)


def _attn_prompt_kernel(lq1, lk1, lq2, lk2, g_ref, q_ref, k_ref, v_ref, o_ref, kb, vb,
                        *, tq, head_dim, lam_init):
    T, vd = q_ref.shape
    nq = T // tq
    lam = _diff_lambda(lq1, lk1, lq2, lk2, lam_init)
    kb[...] = k_ref[...].astype(BF16)
    vb[...] = v_ref[...].astype(BF16)
    first = lax.broadcasted_iota(jnp.int32, (tq, vd), 1) < head_dim
    row = lax.broadcasted_iota(jnp.int32, (tq, tq), 0)
    col = lax.broadcasted_iota(jnp.int32, (tq, tq), 1)
    causal = col <= row
    g = g_ref[...]

    def kv_step(kj, carry, q_parts, masked):
        start = pl.multiple_of(kj * tq, tq)
        kt = kb[pl.ds(start, tq), :]
        vt = vb[pl.ds(start, tq), :]
        new = []
        for c in range(2):
            m, l, acc = carry[c]
            s = lax.dot_general(q_parts[c], kt, (((1,), (1,)), ((), ())),
                                preferred_element_type=F32)
            if masked:
                s = jnp.where(causal, s, -jnp.inf)
            m_new = jnp.maximum(m, jnp.max(s, axis=-1, keepdims=True))
            p = jnp.exp(s - m_new)
            alpha = jnp.exp(m - m_new)
            l = alpha * l + jnp.sum(p, axis=-1, keepdims=True)
            acc = alpha * acc + jnp.dot(p.astype(BF16), vt, preferred_element_type=F32)
            new.append((m_new, l, acc))
        return tuple(new)

    def q_block(qi, _):
        qstart = pl.multiple_of(qi * tq, tq)
        q = q_ref[pl.ds(qstart, tq), :]
        zero = jnp.zeros_like(q)
        q_parts = (jnp.where(first, q, zero), jnp.where(first, zero, q))
        init = tuple((jnp.full((tq, 1), -jnp.inf, F32), jnp.zeros((tq, 1), F32),
                      jnp.zeros((tq, vd), F32)) for _ in range(2))
        carry = lax.fori_loop(0, qi, lambda kj, c: kv_step(kj, c, q_parts, False), init)
        (_, l0, a0), (_, l1, a1) = kv_step(qi, carry, q_parts, True)
        o = a0 / l0 - lam * (a1 / l1)
        o = _rms_rows(o, g, RMS_EPS) * (1.0 - lam_init)
        o_ref[pl.ds(qstart, tq), :] = o.astype(o_ref.dtype)
        return 0

    lax.fori_loop(0, nq, q_block, 0)


def _attn_prompt(q, k, v, lams, g, batch, seq, n_heads, head_dim, lam_init, tq):
    M, width = q.shape
    vd = width // n_heads
    blk = pl.BlockSpec((seq, vd), lambda b, h: (b, h))
    small = pl.BlockSpec((1, head_dim), lambda b, h: (0, 0))
    return pl.pallas_call(
        functools.partial(_attn_prompt_kernel, tq=tq, head_dim=head_dim, lam_init=lam_init),
        name="attn_prompt",
        grid=(batch, n_heads),
        in_specs=[small, small, small, small,
                  pl.BlockSpec((1, vd), lambda b, h: (0, 0)), blk, blk, blk],
        out_specs=blk,
        out_shape=jax.ShapeDtypeStruct((M, width), BF16),
        scratch_shapes=[pltpu.VMEM((seq, vd), BF16), pltpu.VMEM((seq, vd), BF16)],
        compiler_params=_params("parallel", "parallel"),
    )(*[x.reshape(1, head_dim) for x in lams], g.reshape(1, vd), q, k, v)


def _decode_attn_kernel(pt_ref, lq1, lk1, lq2, lk2, g_ref, q_ref, kn_ref, vn_ref, *refs,
                        n_pages, head_dim, lam_init):
    del pt_ref
    k_refs = refs[:n_pages]
    v_refs = refs[n_pages:2 * n_pages]
    o_ref = refs[2 * n_pages]
    m_scr, l_scr, acc_scr = refs[2 * n_pages + 1:]
    c = pl.program_id(1)
    nc = pl.num_programs(1)
    page, n_heads, vd = k_refs[0].shape

    @pl.when(c == 0)
    def _():
        m_scr[...] = jnp.full(m_scr.shape, -jnp.inf, F32)
        l_scr[...] = jnp.zeros(l_scr.shape, F32)
        acc_scr[...] = jnp.zeros(acc_scr.shape, F32)

    q = q_ref[...]
    erow = lax.broadcasted_iota(jnp.int32, (vd, 2 * vd), 0) >= head_dim
    ecol = lax.broadcasted_iota(jnp.int32, (vd, 2 * vd), 1) >= vd
    ones_blk = jnp.where(erow == ecol, 1.0, 0.0).astype(BF16)

    def update(cc, s, vals):
        m_old = m_scr[cc]
        m_new = jnp.maximum(m_old, jnp.max(s, axis=0))
        alpha = jnp.exp(m_old - m_new)
        p = jnp.exp(s - m_new[None])
        l_scr[cc] = alpha * l_scr[cc] + jnp.sum(p, axis=0)
        acc_scr[cc] = alpha * acc_scr[cc] + jnp.sum(p * vals, axis=0)
        m_scr[cc] = m_new

    for gi in range(n_pages):
        kp = k_refs[gi][...]
        vp = v_refs[gi][...]
        prod = (kp * q[None]).reshape(page * n_heads, vd).astype(BF16)
        s = jnp.dot(prod, ones_blk, preferred_element_type=F32)
        for cc in range(2):
            update(cc, s[:, cc * vd:(cc + 1) * vd].reshape(page, n_heads, vd), vp)

    @pl.when(c == nc - 1)
    def _():
        lam = _diff_lambda(lq1, lk1, lq2, lk2, lam_init)
        prod = kn_ref[...] * q
        lane = lax.broadcasted_iota(jnp.int32, prod.shape, 1)
        vn = vn_ref[...]
        outs = []
        for cc in range(2):
            half = (lane >= head_dim) if cc else (lane < head_dim)
            s = jnp.sum(jnp.where(half, prod, 0.0), axis=-1, keepdims=True)
            update(cc, jnp.broadcast_to(s, prod.shape)[None], vn[None])
            outs.append(acc_scr[cc] / l_scr[cc])
        o = outs[0] - lam * outs[1]
        o_ref[...] = _rms_rows(o, g_ref[...], RMS_EPS) * (1.0 - lam_init)


def _decode_attn(q, k_new, v_new, cache_k, cache_v, j, page_table, lams, g,
                 n_heads, head_dim, lam_init, pages_per_step):
    B = q.shape[0]
    vd = 2 * head_dim
    n_pages = page_table.shape[1]
    page = cache_k.shape[2]
    G = pages_per_step
    small = pl.BlockSpec((1, head_dim), lambda b, c, pt: (0, 0))
    per_b = pl.BlockSpec((None, n_heads, vd), lambda b, c, pt: (b, 0, 0))

    def page_spec(gi):
        return pl.BlockSpec((None, None, page, n_heads, vd),
                            lambda b, c, pt: (j, pt[b, c * G + gi], 0, 0, 0))

    grid_spec = pltpu.PrefetchScalarGridSpec(
        num_scalar_prefetch=1,
        grid=(B, n_pages // G),
        in_specs=[small, small, small, small,
                  pl.BlockSpec((1, vd), lambda b, c, pt: (0, 0)),
                  per_b, per_b, per_b]
                 + [page_spec(gi) for gi in range(G)] * 2,
        out_specs=per_b,
        scratch_shapes=[pltpu.VMEM((2, n_heads, vd), F32)] * 3,
    )
    return pl.pallas_call(
        functools.partial(_decode_attn_kernel, n_pages=G, head_dim=head_dim, lam_init=lam_init),
        name="attn_decode",
        grid_spec=grid_spec,
        out_shape=jax.ShapeDtypeStruct((B, n_heads, vd), F32),
        compiler_params=_params("parallel", "arbitrary"),
    )(page_table, *[x.reshape(1, head_dim) for x in lams], g.reshape(1, vd),
      q.reshape(B, n_heads, vd), k_new.reshape(B, n_heads, vd), v_new.reshape(B, n_heads, vd),
      *([cache_k] * G), *([cache_v] * G))


def _mm_res_kernel(*refs, has_bias, emit_x):
    a_ref, w_ref = refs[:2]
    i = 2
    b_ref = None
    if has_bias:
        b_ref = refs[i]
        i += 1
    res_ref, g_ref = refs[i:i + 2]
    outs = refs[i + 2:]
    y = jnp.dot(a_ref[...].astype(BF16), w_ref[...], preferred_element_type=F32)
    if has_bias:
        y = y + b_ref[...]
    x = res_ref[...] + y
    if emit_x:
        outs[0][...] = x
    outs[-1][...] = _rms_rows(x, g_ref[...], RMS_EPS).astype(outs[-1].dtype)


def _mm_res(a, w, bias, res, g, tm, h_dtype, emit_x=True):
    M, K = a.shape
    N = w.shape[1]
    row = lambda m: (m, 0)
    fixed = lambda m: (0, 0)
    ins = [a, w]
    in_specs = [pl.BlockSpec((tm, K), row), pl.BlockSpec((K, N), fixed)]
    if bias is not None:
        ins.append(bias.reshape(1, N))
        in_specs.append(pl.BlockSpec((1, N), fixed))
    ins += [res, g.reshape(1, N)]
    in_specs += [pl.BlockSpec((tm, N), row), pl.BlockSpec((1, N), fixed)]
    out_specs = [pl.BlockSpec((tm, N), row)]
    out_shape = [jax.ShapeDtypeStruct((M, N), h_dtype)]
    if emit_x:
        out_specs = [pl.BlockSpec((tm, N), row)] + out_specs
        out_shape = [jax.ShapeDtypeStruct((M, N), F32)] + out_shape
    out = pl.pallas_call(
        functools.partial(_mm_res_kernel, has_bias=bias is not None, emit_x=emit_x),
        name="mm_res",
        grid=(M // tm,),
        in_specs=in_specs, out_specs=out_specs, out_shape=out_shape,
        compiler_params=_params("parallel"),
    )(*ins)
    return out if emit_x else (None, out[0])


def _glu_kernel(a_ref, w_ref, b_ref, u_ref, *, ch):
    a = a_ref[...]
    lin = jnp.dot(a, w_ref[:, 0:ch], preferred_element_type=F32) + b_ref[:, 0:ch]
    gate = jnp.dot(a, w_ref[:, ch:2 * ch], preferred_element_type=F32) + b_ref[:, ch:2 * ch]
    u_ref[...] = lin * jax.nn.sigmoid(gate)


def _glu(a, w, b, tm):
    M, K = a.shape
    ch = w.shape[1] // 2
    return pl.pallas_call(
        functools.partial(_glu_kernel, ch=ch),
        name="glu",
        grid=(M // tm,),
        in_specs=[pl.BlockSpec((tm, K), lambda m: (m, 0)),
                  pl.BlockSpec(w.shape, lambda m: (0, 0)),
                  pl.BlockSpec((1, 2 * ch), lambda m: (0, 0))],
        out_specs=pl.BlockSpec((tm, ch), lambda m: (m, 0)),
        out_shape=jax.ShapeDtypeStruct((M, ch), F32),
        compiler_params=_params("parallel"),
    )(a, w, b.reshape(1, 2 * ch))


def _conv_prompt_kernel(u_ref, halo_ref, wdw_ref, bdw_ref, lng_ref, lnb_ref, w_ref, b_ref,
                        res_ref, g_ref, x_ref, h_ref, ubuf, cbuf, *, tiles_per_seq, halo, chunk):
    m = pl.program_id(0)
    tm, ch = u_ref.shape
    width = wdw_ref.shape[0]
    seq_start = (m % tiles_per_seq) == 0
    ubuf[0:halo, :] = jnp.where(seq_start, 0.0, halo_ref[...])
    ubuf[halo:halo + tm, :] = u_ref[...]
    off = halo - (width - 1)
    bdw = bdw_ref[...]
    for r in range(tm // chunk):
        acc = jnp.broadcast_to(bdw, (chunk, ch))
        for j in range(width):
            acc = acc + wdw_ref[j:j + 1, :] * ubuf[r * chunk + off + j:r * chunk + off + j + chunk, :]
        cbuf[r * chunk:(r + 1) * chunk, :] = acc
    c = cbuf[...]
    mu = jnp.mean(c, axis=-1, keepdims=True)
    xc = c - mu
    y = xc * lax.rsqrt(jnp.mean(xc * xc, axis=-1, keepdims=True) + LN_EPS)
    y = y * lng_ref[...] + lnb_ref[...]
    y = y * jax.nn.sigmoid(y)
    out = jnp.dot(y.astype(BF16), w_ref[...], preferred_element_type=F32) + b_ref[...]
    x = res_ref[...] + out
    x_ref[...] = x
    h_ref[...] = _rms_rows(x, g_ref[...], RMS_EPS).astype(h_ref.dtype)


def _conv_prompt(u, w_dw, b_dw, ln_g, ln_b, w2, b2, res, g, seq, tm):
    M, ch = u.shape
    width = w_dw.shape[0]
    halo = -(-(width - 1) // SUBLANES) * SUBLANES
    row = lambda m: (m, 0)
    fixed = lambda m: (0, 0)
    vec = pl.BlockSpec((1, ch), fixed)
    return pl.pallas_call(
        functools.partial(_conv_prompt_kernel, tiles_per_seq=seq // tm, halo=halo, chunk=32),
        name="conv_prompt",
        grid=(M // tm,),
        in_specs=[pl.BlockSpec((tm, ch), row),
                  pl.BlockSpec((halo, ch), lambda m: (jnp.maximum(m * (tm // halo) - 1, 0), 0)),
                  pl.BlockSpec((width, ch), fixed), vec, vec, vec,
                  pl.BlockSpec(w2.shape, fixed), vec,
                  pl.BlockSpec((tm, ch), row), vec],
        out_specs=[pl.BlockSpec((tm, ch), row)] * 2,
        out_shape=[jax.ShapeDtypeStruct((M, ch), F32), jax.ShapeDtypeStruct((M, ch), BF16)],
        scratch_shapes=[pltpu.VMEM((halo + tm, ch), F32), pltpu.VMEM((tm, ch), F32)],
        compiler_params=_params("parallel"),
    )(u, u, w_dw, b_dw.reshape(1, ch), ln_g.reshape(1, ch), ln_b.reshape(1, ch), w2,
      b2.reshape(1, ch), res, g.reshape(1, ch))


def _ffn_prompt_kernel(h_ref, wu_a, wu_b, wdw_a, wdw_b, bdw_a, bdw_b, wd_ref, res_ref, g_ref,
                       *refs, tiles_per_seq, emit_x):
    if emit_x:
        x_ref, hn_ref, st_ref, acc, carry, ubuf = refs
    else:
        hn_ref, st_ref, acc, carry, ubuf = refs
    m = pl.program_id(0)
    f = pl.program_id(1)
    nf = pl.num_programs(1)
    tm = h_ref.shape[0]
    pad = SUBLANES
    h = h_ref[...]
    seq_start = (m % tiles_per_seq) == 0
    conv = []
    for half, (wu, wdw, bdw) in enumerate(((wu_a, wdw_a, bdw_a), (wu_b, wdw_b, bdw_b))):
        u = jnp.dot(h, wu[...], preferred_element_type=F32)
        ubuf[half, 0:pad, :] = jnp.where(seq_start, 0.0, carry[f, half])
        ubuf[half, pad:pad + tm, :] = u
        carry[f, half] = u[tm - pad:, :]
        st_ref[0, half] = u[tm - 2:, :]
        cv = (wdw[2:3, :] * u + wdw[1:2, :] * ubuf[half, pad - 1:pad - 1 + tm, :]
              + wdw[0:1, :] * ubuf[half, pad - 2:pad - 2 + tm, :] + bdw[...])
        conv.append(cv)
    gated = (conv[0] * jax.nn.sigmoid(conv[0])) * conv[1]
    y = jnp.dot(gated.astype(BF16), wd_ref[...], preferred_element_type=F32)

    @pl.when(f == 0)
    def _():
        acc[...] = y

    @pl.when(f > 0)
    def _():
        acc[...] += y

    @pl.when(f == nf - 1)
    def _():
        x = res_ref[...] + acc[...]
        if emit_x:
            x_ref[...] = x
        hn_ref[...] = _rms_rows(x, g_ref[...], RMS_EPS).astype(hn_ref.dtype)


def _ffn_prompt(h, w_up, w_dw, b_dw, w_down, res, g, batch, seq, tm, tf, h_dtype, emit_x):
    M, D = h.shape
    d_ff = w_down.shape[0]
    nf = d_ff // tf
    tiles_per_seq = seq // tm
    row = lambda m, f: (m, 0)
    a_col = lambda m, f: (0, f)
    b_col = lambda m, f: (0, f + nf)
    out_specs = [pl.BlockSpec((tm, D), row),
                 pl.BlockSpec((1, 2, 2, tf), lambda m, f: (m, 0, 0, f))]
    out_shape = [jax.ShapeDtypeStruct((M, D), h_dtype),
                 jax.ShapeDtypeStruct((M // tm, 2, 2, d_ff), F32)]
    if emit_x:
        out_specs = [pl.BlockSpec((tm, D), row)] + out_specs
        out_shape = [jax.ShapeDtypeStruct((M, D), F32)] + out_shape
    b2 = b_dw.reshape(1, 2 * d_ff)
    out = pl.pallas_call(
        functools.partial(_ffn_prompt_kernel, tiles_per_seq=tiles_per_seq, emit_x=emit_x),
        name="ffn_prompt",
        grid=(M // tm, nf),
        in_specs=[pl.BlockSpec((tm, D), row),
                  pl.BlockSpec((D, tf), a_col), pl.BlockSpec((D, tf), b_col),
                  pl.BlockSpec((3, tf), a_col), pl.BlockSpec((3, tf), b_col),
                  pl.BlockSpec((1, tf), a_col), pl.BlockSpec((1, tf), b_col),
                  pl.BlockSpec((tf, D), lambda m, f: (f, 0)),
                  pl.BlockSpec((tm, D), row),
                  pl.BlockSpec((1, D), lambda m, f: (0, 0))],
        out_specs=out_specs, out_shape=out_shape,
        scratch_shapes=[pltpu.VMEM((tm, D), F32),
                        pltpu.VMEM((nf, 2, SUBLANES, tf), F32),
                        pltpu.VMEM((2, SUBLANES + tm, tf), F32)],
        compiler_params=_params("arbitrary", "arbitrary"),
    )(h, w_up, w_up, w_dw, w_dw, b2, b2, w_down, res, g.reshape(1, D))
    if emit_x:
        x, hn, st = out
    else:
        (hn, st), x = out, None
    st = st[tiles_per_seq - 1::tiles_per_seq].transpose(0, 2, 1, 3).reshape(batch, 2, 2 * d_ff)
    return x, hn, st


def _mm_plain_kernel(a_ref, w_ref, o_ref):
    o_ref[...] = jnp.dot(a_ref[...], w_ref[...], preferred_element_type=F32)


def _mm_plain(a, w, tn):
    M, K = a.shape
    N = w.shape[1]
    return pl.pallas_call(
        _mm_plain_kernel,
        name="mm_plain",
        grid=(N // tn,),
        in_specs=[pl.BlockSpec((M, K), lambda n: (0, 0)), pl.BlockSpec((K, tn), lambda n: (0, n))],
        out_specs=pl.BlockSpec((M, tn), lambda n: (0, n)),
        out_shape=jax.ShapeDtypeStruct((M, N), F32),
        compiler_params=_params("parallel"),
    )(a, w)


def _ffn_gate_sample_kernel(u_ref, s0_ref, s1_ref, w_ref, b_ref, o_ref, *, d_ff):
    cv = (w_ref[2:3, :] * u_ref[...] + w_ref[1:2, :] * s1_ref[...]
          + w_ref[0:1, :] * s0_ref[...] + b_ref[...])
    a = cv[:, 0:d_ff]
    o_ref[...] = ((a * jax.nn.sigmoid(a)) * cv[:, d_ff:2 * d_ff]).astype(o_ref.dtype)


def _ffn_gate_sample(u, state, w_dw, b_dw):
    B, two_ff = u.shape
    d_ff = two_ff // 2
    full = lambda s: pl.BlockSpec(s, lambda i: (0, 0))
    return pl.pallas_call(
        functools.partial(_ffn_gate_sample_kernel, d_ff=d_ff),
        name="ffn_gate_sample",
        grid=(1,),
        in_specs=[full((B, two_ff))] * 3 + [full((3, two_ff)), full((1, two_ff))],
        out_specs=full((B, d_ff)),
        out_shape=jax.ShapeDtypeStruct((B, d_ff), BF16),
        compiler_params=_params("arbitrary"),
    )(u, state[:, 0], state[:, 1], w_dw, b_dw.reshape(1, two_ff))


def _conv_sample_kernel(u_ref, st_ref, wdw_ref, bdw_ref, lng_ref, lnb_ref, o_ref):
    width = wdw_ref.shape[0]
    c = jnp.sum(st_ref[...] * wdw_ref[0:width - 1, :][None], axis=1)
    c = c + u_ref[...] * wdw_ref[width - 1:width, :] + bdw_ref[...]
    mu = jnp.mean(c, axis=-1, keepdims=True)
    xc = c - mu
    y = xc * lax.rsqrt(jnp.mean(xc * xc, axis=-1, keepdims=True) + LN_EPS)
    y = y * lng_ref[...] + lnb_ref[...]
    o_ref[...] = (y * jax.nn.sigmoid(y)).astype(o_ref.dtype)


def _conv_sample(u, state, w_dw, b_dw, ln_g, ln_b):
    B, ch = u.shape
    full2 = lambda s: pl.BlockSpec(s, lambda i: (0, 0))
    vec = full2((1, ch))
    return pl.pallas_call(
        _conv_sample_kernel,
        name="conv_sample",
        grid=(1,),
        in_specs=[full2((B, ch)), pl.BlockSpec(state.shape, lambda i: (0, 0, 0)),
                  full2(w_dw.shape), vec, vec, vec],
        out_specs=full2((B, ch)),
        out_shape=jax.ShapeDtypeStruct((B, ch), BF16),
        compiler_params=_params("arbitrary"),
    )(u, state, w_dw, b_dw.reshape(1, ch), ln_g.reshape(1, ch), ln_b.reshape(1, ch))


def kernel(x_prompt, x_sample, cache_k, cache_v, page_table, state_conv, state_ffn, norm_mix, norm_ffn, norm_final, w_qkv, w_o, lambda_q1, lambda_k1, lambda_q2, lambda_k2, subln_g, w_pw1, b_pw1, w_dw, b_dw, ln_g, ln_b, w_pw2, b_pw2, w_up, w_ffn_dw, b_ffn_dw, w_down):
    Bp, T, D = x_prompt.shape
    Bs, Ts, _ = x_sample.shape
    assert Ts == 1, "sample group is one token per sequence"
    depth = norm_mix.shape[0]
    head_dim = lambda_q1.shape[1]
    vd = subln_g.shape[1]
    n_heads = w_o.shape[1] // vd
    scale = head_dim ** -0.5
    Mp, Ms = Bp * T, Bs * Ts
    TM = 512

    xp = x_prompt.reshape(Mp, D)
    xs = x_sample.reshape(Ms, D)
    hp = _rmsnorm(xp, norm_mix[0], TM, BF16)
    hs = _rmsnorm(xs, norm_mix[0], Ms, BF16)

    kp_l, vp_l, ks_l, vs_l = [], [], [], []
    cp_l, cs_l, fp_l, fs_l = [], [], [], []
    for i in range(depth):
        j = i // N_MIXERS
        if i % N_MIXERS == 0:
            lam_init = 0.8 - 0.6 * math.exp(-0.3 * i)
            lams = (lambda_q1[j], lambda_k1[j], lambda_q2[j], lambda_k2[j])
            wq = w_qkv[j].astype(BF16)
            wo = w_o[j].astype(BF16)
            qp, kp, vp = _qkv(hp, wq, TM, scale, BF16)
            qs, ks, vs = _qkv(hs, wq, Ms, scale, F32)
            op = _attn_prompt(qp, kp, vp, lams, subln_g[j], Bp, T, n_heads, head_dim, lam_init, 256)
            os_ = _decode_attn(qs, ks, vs, cache_k, cache_v, j, page_table, lams, subln_g[j],
                               n_heads, head_dim, lam_init, 8)
            xp, hp = _mm_res(op, wo, None, xp, norm_ffn[i], TM, BF16)
            xs, hs = _mm_res(os_.reshape(Ms, n_heads * vd), wo, None, xs, norm_ffn[i], Ms, BF16)
            kp_l.append(kp.reshape(Bp, T, n_heads, vd))
            vp_l.append(vp.reshape(Bp, T, n_heads, vd))
            ks_l.append(ks.reshape(Bs, Ts, n_heads, vd))
            vs_l.append(vs.reshape(Bs, Ts, n_heads, vd))
        else:
            w1 = w_pw1[j].astype(BF16)
            w2 = w_pw2[j].astype(BF16)
            width = w_dw.shape[1]
            up = _glu(hp, w1, b_pw1[j], TM)
            us = _glu(hs, w1, b_pw1[j], Ms)
            cs = _conv_sample(us, state_conv[j], w_dw[j], b_dw[j], ln_g[j], ln_b[j])
            xp, hp = _conv_prompt(up, w_dw[j], b_dw[j], ln_g[j], ln_b[j], w2, b_pw2[j], xp,
                                  norm_ffn[i], T, 256)
            xs, hs = _mm_res(cs, w2, b_pw2[j], xs, norm_ffn[i], Ms, BF16)
            cp_l.append(up.reshape(Bp, T, -1)[:, T - (width - 1):])
            cs_l.append(jnp.concatenate([state_conv[j][:, 1:], us[:, None]], axis=1))
        last = i == depth - 1
        g_next = norm_final if last else norm_mix[i + 1]
        h_dtype = F32 if last else BF16
        wu = w_up[i].astype(BF16)
        wd = w_down[i].astype(BF16)
        d_ff = wd.shape[0]
        xp, hp, fbp = _ffn_prompt(hp, wu, w_ffn_dw[i], b_ffn_dw[i], wd, xp, g_next, Bp, T,
                                  TM, d_ff // 2, h_dtype, not last)
        u_s = _mm_plain(hs, wu, d_ff // 2)
        gs = _ffn_gate_sample(u_s, state_ffn[i], w_ffn_dw[i], b_ffn_dw[i])
        xs, hs = _mm_res(gs, wd, None, xs, g_next, Ms, h_dtype, emit_x=not last)
        fp_l.append(fbp)
        fs_l.append(jnp.concatenate([state_ffn[i][:, 1:], u_s[:, None]], axis=1))

    y_prompt = hp.reshape(Bp, T, D)
    y_sample = hs.reshape(Bs, Ts, D)
    return (y_prompt, y_sample, jnp.stack(kp_l), jnp.stack(vp_l), jnp.stack(ks_l), jnp.stack(vs_l),
            jnp.stack(cp_l), jnp.stack(cs_l), jnp.stack(fp_l), jnp.stack(fs_l))
```

```python
import functools
import math

import jax
import jax.numpy as jnp
from jax import lax
from jax.experimental import pallas as pl
from jax.experimental.pallas import tpu as pltpu

F32 = jnp.float32
BF16 = jnp.bfloat16

RMS_EPS = 1e-6
LN_EPS = 1e-5
N_MIXERS = 2
LANES = 128
SUBLANES = 8
VMEM_LIMIT = 56 * 1024 * 1024


def _params(*sem):
    return pltpu.CompilerParams(dimension_semantics=sem, vmem_limit_bytes=VMEM_LIMIT)


def _rms_rows(x, g, eps):
    ms = jnp.mean(x * x, axis=-1, keepdims=True)
    return x * lax.rsqrt(ms + eps) * g


def _diff_lambda(lq1, lk1, lq2, lk2, lam_init):
    a = jnp.sum(lq1[...] * lk1[...], keepdims=True)
    b = jnp.sum(lq2[...] * lk2[...], keepdims=True)
    return jnp.exp(a) - jnp.exp(b) + lam_init


def _qkv_kernel(x_ref, g_ref, w_ref, q_ref, k_ref, v_ref, *, width, scale):
    h = _rms_rows(x_ref[...], g_ref[...], RMS_EPS).astype(BF16)
    q = jnp.dot(h, w_ref[:, 0:width], preferred_element_type=F32)
    q_ref[...] = (q * scale).astype(q_ref.dtype)
    k_ref[...] = jnp.dot(h, w_ref[:, width:2 * width], preferred_element_type=F32)
    v_ref[...] = jnp.dot(h, w_ref[:, 2 * width:3 * width], preferred_element_type=F32)


def _qkv(x, g, w, tm, scale, q_dtype):
    M, D = x.shape
    width = w.shape[1] // 3
    blk = lambda m: (m, 0)
    return pl.pallas_call(
        functools.partial(_qkv_kernel, width=width, scale=scale),
        name="qkv",
        grid=(M // tm,),
        in_specs=[pl.BlockSpec((tm, D), blk),
                  pl.BlockSpec((1, D), lambda m: (0, 0)),
                  pl.BlockSpec(w.shape, lambda m: (0, 0))],
        out_specs=[pl.BlockSpec((tm, width), blk)] * 3,
        out_shape=[jax.ShapeDtypeStruct((M, width), q_dtype),
                   jax.ShapeDtypeStruct((M, width), F32),
                   jax.ShapeDtypeStruct((M, width), F32)],
        compiler_params=_params("parallel"),
    )(x, g.reshape(1, D), w)


def _attn_prompt_kernel(lq1, lk1, lq2, lk2, g_ref, q_ref, k_ref, v_ref, o_ref, kb, vb,
                        *, tq, head_dim, lam_init):
    T, vd = q_ref.shape
    nq = T // tq
    lam = _diff_lambda(lq1, lk1, lq2, lk2, lam_init)
    kb[...] = k_ref[...].astype(BF16)
    vb[...] = v_ref[...].astype(BF16)
    first = lax.broadcasted_iota(jnp.int32, (tq, vd), 1) < head_dim
    row = lax.broadcasted_iota(jnp.int32, (tq, tq), 0)
    col = lax.broadcasted_iota(jnp.int32, (tq, tq), 1)
    causal = col <= row
    g = g_ref[...]

    def kv_step(kj, carry, q_parts, masked):
        start = pl.multiple_of(kj * tq, tq)
        kt = kb[pl.ds(start, tq), :]
        vt = vb[pl.ds(start, tq), :]
        new = []
        for c in range(2):
            m, l, acc = carry[c]
            s = lax.dot_general(q_parts[c], kt, (((1,), (1,)), ((), ())),
                                preferred_element_type=F32)
            if masked:
                s = jnp.where(causal, s, -jnp.inf)
            m_new = jnp.maximum(m, jnp.max(s, axis=-1, keepdims=True))
            p = jnp.exp2(s - m_new)
            alpha = jnp.exp2(m - m_new)
            l = alpha * l + jnp.sum(p, axis=-1, keepdims=True)
            acc = alpha * acc + jnp.dot(p.astype(BF16), vt, preferred_element_type=F32)
            new.append((m_new, l, acc))
        return tuple(new)

    def q_block(qi, _):
        qstart = pl.multiple_of(qi * tq, tq)
        q = q_ref[pl.ds(qstart, tq), :]
        zero = jnp.zeros_like(q)
        q_parts = (jnp.where(first, q, zero), jnp.where(first, zero, q))
        init = tuple((jnp.full((tq, 1), -jnp.inf, F32), jnp.zeros((tq, 1), F32),
                      jnp.zeros((tq, vd), F32)) for _ in range(2))
        carry = lax.fori_loop(0, qi, lambda kj, c: kv_step(kj, c, q_parts, False), init)
        (_, l0, a0), (_, l1, a1) = kv_step(qi, carry, q_parts, True)
        o = a0 / l0 - lam * (a1 / l1)
        o = _rms_rows(o, g, RMS_EPS) * (1.0 - lam_init)
        o_ref[pl.ds(qstart, tq), :] = o.astype(o_ref.dtype)
        return 0

    lax.fori_loop(0, nq, q_block, 0)


def _attn_prompt(q, k, v, lams, g, batch, seq, n_heads, head_dim, lam_init, tq):
    M, width = q.shape
    vd = width // n_heads
    blk = pl.BlockSpec((seq, vd), lambda b, h: (b, h))
    small = pl.BlockSpec((1, head_dim), lambda b, h: (0, 0))
    return pl.pallas_call(
        functools.partial(_attn_prompt_kernel, tq=tq, head_dim=head_dim, lam_init=lam_init),
        name="attn_prompt",
        grid=(batch, n_heads),
        in_specs=[small, small, small, small,
                  pl.BlockSpec((1, vd), lambda b, h: (0, 0)), blk, blk, blk],
        out_specs=blk,
        out_shape=jax.ShapeDtypeStruct((M, width), BF16),
        scratch_shapes=[pltpu.VMEM((seq, vd), BF16), pltpu.VMEM((seq, vd), BF16)],
        compiler_params=_params("parallel", "parallel"),
    )(*[x.reshape(1, head_dim) for x in lams], g.reshape(1, vd), q, k, v)


def _decode_attn_kernel(pt_ref, lq1, lk1, lq2, lk2, g_ref, q_ref, kn_ref, vn_ref, *refs,
                        n_pages, n_heads, head_dim, lam_init):
    del pt_ref
    k_refs = refs[:n_pages]
    v_refs = refs[n_pages:2 * n_pages]
    o_ref = refs[2 * n_pages]
    m_scr, l_scr, acc_scr = refs[2 * n_pages + 1:]
    c = pl.program_id(1)
    nc = pl.num_programs(1)
    rows_per_page, vd = k_refs[0].shape
    n_rows = 2 * n_heads

    @pl.when(c == 0)
    def _():
        m_scr[...] = jnp.full(m_scr.shape, -jnp.inf, F32)
        l_scr[...] = jnp.zeros(l_scr.shape, F32)
        acc_scr[...] = jnp.zeros(acc_scr.shape, F32)

    q = q_ref[...]
    lane = lax.broadcasted_iota(jnp.int32, q.shape, 1)
    q_rows = jnp.concatenate([jnp.where(lane < head_dim, q, 0.0),
                              jnp.where(lane < head_dim, 0.0, q)], axis=0).astype(BF16)
    assert n_heads & (n_heads - 1) == 0
    row_head = lax.broadcasted_iota(jnp.int32, (n_rows, rows_per_page), 0) & (n_heads - 1)
    col_head = lax.broadcasted_iota(jnp.int32, (n_rows, rows_per_page), 1) & (n_heads - 1)
    same_head = row_head == col_head

    scores = []
    for gi in range(n_pages):
        s = lax.dot_general(q_rows, k_refs[gi][...].astype(BF16), (((1,), (1,)), ((), ())),
                            preferred_element_type=F32)
        scores.append(jnp.where(same_head, s, -jnp.inf))
    m_old = m_scr[...]
    m_new = m_old
    for s in scores:
        m_new = jnp.maximum(m_new, jnp.max(s, axis=-1, keepdims=True))
    alpha = jnp.exp2(m_old - m_new)
    l_new = alpha * l_scr[...]
    acc = alpha * acc_scr[...]
    for gi in range(n_pages):
        p = jnp.exp2(scores[gi] - m_new)
        l_new = l_new + jnp.sum(p, axis=-1, keepdims=True)
        acc = acc + jnp.dot(p.astype(BF16), v_refs[gi][...].astype(BF16),
                            preferred_element_type=F32)
    m_scr[...] = m_new
    l_scr[...] = l_new
    acc_scr[...] = acc

    @pl.when(c == nc - 1)
    def _():
        lam = _diff_lambda(lq1, lk1, lq2, lk2, lam_init)
        kn2 = jnp.concatenate([kn_ref[...]] * 2, axis=0)
        vn2 = jnp.concatenate([vn_ref[...]] * 2, axis=0)
        s_new = jnp.sum(kn2 * q_rows.astype(F32), axis=-1, keepdims=True)
        m_fin = jnp.maximum(m_new, s_new)
        a_fin = jnp.exp2(m_new - m_fin)
        p_new = jnp.exp2(s_new - m_fin)
        l_fin = a_fin * l_new + p_new
        o2 = (a_fin * acc + p_new * vn2) / l_fin
        o = o2[0:n_heads] - lam * o2[n_heads:n_rows]
        o_ref[...] = _rms_rows(o, g_ref[...], RMS_EPS) * (1.0 - lam_init)


def _decode_attn(q, k_new, v_new, cache_k, cache_v, j, page_table, lams, g,
                 n_heads, head_dim, lam_init, pages_per_step):
    B = q.shape[0]
    vd = 2 * head_dim
    n_pages = page_table.shape[1]
    n_layers, n_pool, page = cache_k.shape[:3]
    G = pages_per_step
    small = pl.BlockSpec((1, head_dim), lambda b, c, pt: (0, 0))
    per_b = pl.BlockSpec((None, n_heads, vd), lambda b, c, pt: (b, 0, 0))
    ck = cache_k.reshape(n_layers, n_pool, page * n_heads, vd)
    cv = cache_v.reshape(n_layers, n_pool, page * n_heads, vd)

    def page_spec(gi):
        return pl.BlockSpec((None, None, page * n_heads, vd),
                            lambda b, c, pt: (j, pt[b, c * G + gi], 0, 0))

    grid_spec = pltpu.PrefetchScalarGridSpec(
        num_scalar_prefetch=1,
        grid=(B, n_pages // G),
        in_specs=[small, small, small, small,
                  pl.BlockSpec((1, vd), lambda b, c, pt: (0, 0)),
                  per_b, per_b, per_b]
                 + [page_spec(gi) for gi in range(G)] * 2,
        out_specs=per_b,
        scratch_shapes=[pltpu.VMEM((2 * n_heads, 1), F32), pltpu.VMEM((2 * n_heads, 1), F32),
                        pltpu.VMEM((2 * n_heads, vd), F32)],
    )
    return pl.pallas_call(
        functools.partial(_decode_attn_kernel, n_pages=G, n_heads=n_heads, head_dim=head_dim,
                          lam_init=lam_init),
        name="attn_decode",
        grid_spec=grid_spec,
        out_shape=jax.ShapeDtypeStruct((B, n_heads, vd), F32),
        compiler_params=_params("parallel", "arbitrary"),
    )(page_table, *[x.reshape(1, head_dim) for x in lams], g.reshape(1, vd),
      q.reshape(B, n_heads, vd), k_new.reshape(B, n_heads, vd), v_new.reshape(B, n_heads, vd),
      *([ck] * G), *([cv] * G))


def _mm_res_kernel(*refs, has_bias, emit_x):
    a_ref, w_ref = refs[:2]
    i = 2
    b_ref = None
    if has_bias:
        b_ref = refs[i]
        i += 1
    res_ref, g_ref = refs[i:i + 2]
    outs = refs[i + 2:]
    y = jnp.dot(a_ref[...].astype(BF16), w_ref[...], preferred_element_type=F32)
    if has_bias:
        y = y + b_ref[...]
    x = res_ref[...] + y
    if emit_x:
        outs[0][...] = x
    outs[-1][...] = _rms_rows(x, g_ref[...], RMS_EPS).astype(outs[-1].dtype)


def _mm_res(a, w, bias, res, g, tm, h_dtype, emit_x=True):
    M, K = a.shape
    N = w.shape[1]
    row = lambda m: (m, 0)
    fixed = lambda m: (0, 0)
    ins = [a, w]
    in_specs = [pl.BlockSpec((tm, K), row), pl.BlockSpec((K, N), fixed)]
    if bias is not None:
        ins.append(bias.reshape(1, N))
        in_specs.append(pl.BlockSpec((1, N), fixed))
    ins += [res, g.reshape(1, N)]
    in_specs += [pl.BlockSpec((tm, N), row), pl.BlockSpec((1, N), fixed)]
    out_specs = [pl.BlockSpec((tm, N), row)]
    out_shape = [jax.ShapeDtypeStruct((M, N), h_dtype)]
    if emit_x:
        out_specs = [pl.BlockSpec((tm, N), row)] + out_specs
        out_shape = [jax.ShapeDtypeStruct((M, N), F32)] + out_shape
    out = pl.pallas_call(
        functools.partial(_mm_res_kernel, has_bias=bias is not None, emit_x=emit_x),
        name="mm_res",
        grid=(M // tm,),
        in_specs=in_specs, out_specs=out_specs, out_shape=out_shape,
        compiler_params=_params("parallel"),
    )(*ins)
    return out if emit_x else (None, out[0])


def _glu_kernel(a_ref, w_ref, b_ref, u_ref, *, ch):
    a = a_ref[...]
    lin = jnp.dot(a, w_ref[:, 0:ch], preferred_element_type=F32) + b_ref[:, 0:ch]
    gate = jnp.dot(a, w_ref[:, ch:2 * ch], preferred_element_type=F32) + b_ref[:, ch:2 * ch]
    u_ref[...] = lin * jax.nn.sigmoid(gate)


def _glu(a, w, b, tm):
    M, K = a.shape
    ch = w.shape[1] // 2
    return pl.pallas_call(
        functools.partial(_glu_kernel, ch=ch),
        name="glu",
        grid=(M // tm,),
        in_specs=[pl.BlockSpec((tm, K), lambda m: (m, 0)),
                  pl.BlockSpec(w.shape, lambda m: (0, 0)),
                  pl.BlockSpec((1, 2 * ch), lambda m: (0, 0))],
        out_specs=pl.BlockSpec((tm, ch), lambda m: (m, 0)),
        out_shape=jax.ShapeDtypeStruct((M, ch), F32),
        compiler_params=_params("parallel"),
    )(a, w, b.reshape(1, 2 * ch))


def _conv_prompt_kernel(u_ref, halo_ref, wdw_ref, bdw_ref, lng_ref, lnb_ref, w_ref, b_ref,
                        res_ref, g_ref, x_ref, h_ref, ubuf, cbuf, wb, *, tiles_per_seq, halo, chunk):
    m = pl.program_id(0)
    tm, ch = u_ref.shape
    width = wdw_ref.shape[0]
    seq_start = (m % tiles_per_seq) == 0
    ubuf[0, 0:halo, :] = jnp.where(seq_start, 0.0, halo_ref[...])
    ubuf[0, halo:halo + tm, :] = u_ref[...]
    n_rows = halo + tm
    base = ubuf[0]
    for r in range(1, SUBLANES):
        ubuf[r] = pltpu.roll(base, n_rows - r, axis=0)
    off = halo - (width - 1)
    wb[...] = jnp.broadcast_to(wdw_ref[...][:, None, :], wb.shape)
    bdw = bdw_ref[...]
    groups = chunk // SUBLANES
    for c0 in range(0, tm, chunk):
        acc = jnp.broadcast_to(bdw, (groups, SUBLANES, ch))
        for j in range(width):
            a, r = divmod(off + j, SUBLANES)
            win = ubuf[r, c0 + a * SUBLANES:c0 + a * SUBLANES + chunk, :]
            acc = acc + wb[j][None] * win.reshape(groups, SUBLANES, ch)
        cbuf[c0:c0 + chunk, :] = acc.reshape(chunk, ch)
    c = cbuf[...]
    mu = jnp.mean(c, axis=-1, keepdims=True)
    xc = c - mu
    y = xc * lax.rsqrt(jnp.mean(xc * xc, axis=-1, keepdims=True) + LN_EPS)
    y = y * lng_ref[...] + lnb_ref[...]
    y = y * jax.nn.sigmoid(y)
    out = jnp.dot(y.astype(BF16), w_ref[...], preferred_element_type=F32) + b_ref[...]
    x = res_ref[...] + out
    x_ref[...] = x
    h_ref[...] = _rms_rows(x, g_ref[...], RMS_EPS).astype(h_ref.dtype)


def _conv_prompt(u, w_dw, b_dw, ln_g, ln_b, w2, b2, res, g, seq, tm):
    M, ch = u.shape
    width = w_dw.shape[0]
    halo = -(-(width - 1) // SUBLANES) * SUBLANES
    row = lambda m: (m, 0)
    fixed = lambda m: (0, 0)
    vec = pl.BlockSpec((1, ch), fixed)
    return pl.pallas_call(
        functools.partial(_conv_prompt_kernel, tiles_per_seq=seq // tm, halo=halo, chunk=32),
        name="conv_prompt",
        grid=(M // tm,),
        in_specs=[pl.BlockSpec((tm, ch), row),
                  pl.BlockSpec((halo, ch), lambda m: (jnp.maximum(m * (tm // halo) - 1, 0), 0)),
                  pl.BlockSpec((width, ch), fixed), vec, vec, vec,
                  pl.BlockSpec(w2.shape, fixed), vec,
                  pl.BlockSpec((tm, ch), row), vec],
        out_specs=[pl.BlockSpec((tm, ch), row)] * 2,
        out_shape=[jax.ShapeDtypeStruct((M, ch), F32), jax.ShapeDtypeStruct((M, ch), BF16)],
        scratch_shapes=[pltpu.VMEM((SUBLANES, halo + tm, ch), F32), pltpu.VMEM((tm, ch), F32),
                        pltpu.VMEM((width, SUBLANES, ch), F32)],
        compiler_params=_params("parallel"),
    )(u, u, w_dw, b_dw.reshape(1, ch), ln_g.reshape(1, ch), ln_b.reshape(1, ch), w2,
      b2.reshape(1, ch), res, g.reshape(1, ch))


def _ffn_prompt_kernel(h_ref, wu_ref, wdw_ref, bdw_ref, wd_ref, res_ref, g_ref,
                       *refs, tiles_per_seq, emit_x, sub):
    if emit_x:
        x_ref, hn_ref, st_ref, carry, gbuf = refs
    else:
        hn_ref, st_ref, carry, gbuf = refs
    m = pl.program_id(0)
    tm = h_ref.shape[0]
    d_ff = wd_ref.shape[0]
    pad = SUBLANES
    h = h_ref[...]
    seq_start = (m % tiles_per_seq) == 0
    for c0 in range(0, d_ff, sub):
        conv = []
        for half in range(2):
            cols = slice(half * d_ff + c0, half * d_ff + c0 + sub)
            u = jnp.dot(h, wu_ref[:, cols], preferred_element_type=F32)
            prev = jnp.where(seq_start, 0.0, carry[:, cols])
            carry[:, cols] = u[tm - pad:, :]
            st_ref[0, half, :, c0:c0 + sub] = u[tm - 2:, :]
            ext = jnp.concatenate([prev, u], axis=0)
            conv.append(wdw_ref[2:3, cols] * u + wdw_ref[1:2, cols] * ext[pad - 1:pad - 1 + tm]
                        + wdw_ref[0:1, cols] * ext[pad - 2:pad - 2 + tm] + bdw_ref[:, cols])
        gbuf[:, c0:c0 + sub] = ((conv[0] * jax.nn.sigmoid(conv[0])) * conv[1]).astype(BF16)
    y = jnp.dot(gbuf[...], wd_ref[...], preferred_element_type=F32)
    x = res_ref[...] + y
    if emit_x:
        x_ref[...] = x
    hn_ref[...] = _rms_rows(x, g_ref[...], RMS_EPS).astype(hn_ref.dtype)


def _ffn_prompt(h, w_up, w_dw, b_dw, w_down, res, g, batch, seq, tm, sub, h_dtype, emit_x):
    M, D = h.shape
    d_ff = w_down.shape[0]
    tiles_per_seq = seq // tm
    row = lambda m: (m, 0)
    fixed = lambda m: (0, 0)
    once = dict(pipeline_mode=pl.Buffered(1))
    out_specs = [pl.BlockSpec((tm, D), row),
                 pl.BlockSpec((1, 2, 2, d_ff), lambda m: (m, 0, 0, 0))]
    out_shape = [jax.ShapeDtypeStruct((M, D), h_dtype),
                 jax.ShapeDtypeStruct((M // tm, 2, 2, d_ff), F32)]
    if emit_x:
        out_specs = [pl.BlockSpec((tm, D), row)] + out_specs
        out_shape = [jax.ShapeDtypeStruct((M, D), F32)] + out_shape
    out = pl.pallas_call(
        functools.partial(_ffn_prompt_kernel, tiles_per_seq=tiles_per_seq, emit_x=emit_x, sub=sub),
        name="ffn_prompt",
        grid=(M // tm,),
        in_specs=[pl.BlockSpec((tm, D), row),
                  pl.BlockSpec((D, 2 * d_ff), fixed, **once),
                  pl.BlockSpec((3, 2 * d_ff), fixed, **once),
                  pl.BlockSpec((1, 2 * d_ff), fixed, **once),
                  pl.BlockSpec((d_ff, D), fixed, **once),
                  pl.BlockSpec((tm, D), row),
                  pl.BlockSpec((1, D), fixed, **once)],
        out_specs=out_specs, out_shape=out_shape,
        scratch_shapes=[pltpu.VMEM((SUBLANES, 2 * d_ff), F32), pltpu.VMEM((tm, d_ff), BF16)],
        compiler_params=_params("arbitrary"),
    )(h, w_up, w_dw, b_dw.reshape(1, 2 * d_ff), w_down, res, g.reshape(1, D))
    if emit_x:
        x, hn, st = out
    else:
        (hn, st), x = out, None
    st = st[tiles_per_seq - 1::tiles_per_seq].transpose(0, 2, 1, 3).reshape(batch, 2, 2 * d_ff)
    return x, hn, st


def _mm_plain_kernel(a_ref, w_ref, o_ref):
    o_ref[...] = jnp.dot(a_ref[...], w_ref[...], preferred_element_type=F32)


def _mm_plain(a, w, tn):
    M, K = a.shape
    N = w.shape[1]
    return pl.pallas_call(
        _mm_plain_kernel,
        name="mm_plain",
        grid=(N // tn,),
        in_specs=[pl.BlockSpec((M, K), lambda n: (0, 0)), pl.BlockSpec((K, tn), lambda n: (0, n))],
        out_specs=pl.BlockSpec((M, tn), lambda n: (0, n)),
        out_shape=jax.ShapeDtypeStruct((M, N), F32),
        compiler_params=_params("parallel"),
    )(a, w)


def _ffn_gate_sample_kernel(u_ref, s0_ref, s1_ref, w_ref, b_ref, o_ref, *, d_ff):
    cv = (w_ref[2:3, :] * u_ref[...] + w_ref[1:2, :] * s1_ref[...]
          + w_ref[0:1, :] * s0_ref[...] + b_ref[...])
    a = cv[:, 0:d_ff]
    o_ref[...] = ((a * jax.nn.sigmoid(a)) * cv[:, d_ff:2 * d_ff]).astype(o_ref.dtype)


def _ffn_gate_sample(u, state, w_dw, b_dw):
    B, two_ff = u.shape
    d_ff = two_ff // 2
    full = lambda s: pl.BlockSpec(s, lambda i: (0, 0))
    return pl.pallas_call(
        functools.partial(_ffn_gate_sample_kernel, d_ff=d_ff),
        name="ffn_gate_sample",
        grid=(1,),
        in_specs=[full((B, two_ff))] * 3 + [full((3, two_ff)), full((1, two_ff))],
        out_specs=full((B, d_ff)),
        out_shape=jax.ShapeDtypeStruct((B, d_ff), BF16),
        compiler_params=_params("arbitrary"),
    )(u, state[:, 0], state[:, 1], w_dw, b_dw.reshape(1, two_ff))


def _conv_sample_kernel(u_ref, st_ref, wdw_ref, bdw_ref, lng_ref, lnb_ref, o_ref):
    width = wdw_ref.shape[0]
    c = jnp.sum(st_ref[...] * wdw_ref[0:width - 1, :][None], axis=1)
    c = c + u_ref[...] * wdw_ref[width - 1:width, :] + bdw_ref[...]
    mu = jnp.mean(c, axis=-1, keepdims=True)
    xc = c - mu
    y = xc * lax.rsqrt(jnp.mean(xc * xc, axis=-1, keepdims=True) + LN_EPS)
    y = y * lng_ref[...] + lnb_ref[...]
    o_ref[...] = (y * jax.nn.sigmoid(y)).astype(o_ref.dtype)


def _conv_sample(u, state, w_dw, b_dw, ln_g, ln_b):
    B, ch = u.shape
    full2 = lambda s: pl.BlockSpec(s, lambda i: (0, 0))
    vec = full2((1, ch))
    return pl.pallas_call(
        _conv_sample_kernel,
        name="conv_sample",
        grid=(1,),
        in_specs=[full2((B, ch)), pl.BlockSpec(state.shape, lambda i: (0, 0, 0)),
                  full2(w_dw.shape), vec, vec, vec],
        out_specs=full2((B, ch)),
        out_shape=jax.ShapeDtypeStruct((B, ch), BF16),
        compiler_params=_params("arbitrary"),
    )(u, state, w_dw, b_dw.reshape(1, ch), ln_g.reshape(1, ch), ln_b.reshape(1, ch))


def kernel(x_prompt, x_sample, cache_k, cache_v, page_table, state_conv, state_ffn, norm_mix, norm_ffn, norm_final, w_qkv, w_o, lambda_q1, lambda_k1, lambda_q2, lambda_k2, subln_g, w_pw1, b_pw1, w_dw, b_dw, ln_g, ln_b, w_pw2, b_pw2, w_up, w_ffn_dw, b_ffn_dw, w_down):
    Bp, T, D = x_prompt.shape
    Bs, Ts, _ = x_sample.shape
    assert Ts == 1, "sample group is one token per sequence"
    depth = norm_mix.shape[0]
    head_dim = lambda_q1.shape[1]
    vd = subln_g.shape[1]
    n_heads = w_o.shape[1] // vd
    scale = head_dim ** -0.5 * math.log2(math.e)
    Mp, Ms = Bp * T, Bs * Ts
    TM = 512

    xp = x_prompt.reshape(Mp, D)
    xs = x_sample.reshape(Ms, D)
    hp = hs = None

    kp_l, vp_l, ks_l, vs_l = [], [], [], []
    cp_l, cs_l, fp_l, fs_l = [], [], [], []
    for i in range(depth):
        j = i // N_MIXERS
        if i % N_MIXERS == 0:
            lam_init = 0.8 - 0.6 * math.exp(-0.3 * i)
            lams = (lambda_q1[j], lambda_k1[j], lambda_q2[j], lambda_k2[j])
            wq = w_qkv[j].astype(BF16)
            wo = w_o[j].astype(BF16)
            qp, kp, vp = _qkv(xp, norm_mix[i], wq, TM, scale, BF16)
            qs, ks, vs = _qkv(xs, norm_mix[i], wq, Ms, scale, F32)
            op = _attn_prompt(qp, kp, vp, lams, subln_g[j], Bp, T, n_heads, head_dim, lam_init, 512)
            os_ = _decode_attn(qs, ks, vs, cache_k, cache_v, j, page_table, lams, subln_g[j],
                               n_heads, head_dim, lam_init, 8)
            xp, hp = _mm_res(op, wo, None, xp, norm_ffn[i], TM, BF16)
            xs, hs = _mm_res(os_.reshape(Ms, n_heads * vd), wo, None, xs, norm_ffn[i], Ms, BF16)
            kp_l.append(kp.reshape(Bp, T, n_heads, vd))
            vp_l.append(vp.reshape(Bp, T, n_heads, vd))
            ks_l.append(ks.reshape(Bs, Ts, n_heads, vd))
            vs_l.append(vs.reshape(Bs, Ts, n_heads, vd))
        else:
            w1 = w_pw1[j].astype(BF16)
            w2 = w_pw2[j].astype(BF16)
            width = w_dw.shape[1]
            up = _glu(hp, w1, b_pw1[j], TM)
            us = _glu(hs, w1, b_pw1[j], Ms)
            cs = _conv_sample(us, state_conv[j], w_dw[j], b_dw[j], ln_g[j], ln_b[j])
            xp, hp = _conv_prompt(up, w_dw[j], b_dw[j], ln_g[j], ln_b[j], w2, b_pw2[j], xp,
                                  norm_ffn[i], T, 256)
            xs, hs = _mm_res(cs, w2, b_pw2[j], xs, norm_ffn[i], Ms, BF16)
            cp_l.append(up.reshape(Bp, T, -1)[:, T - (width - 1):])
            cs_l.append(jnp.concatenate([state_conv[j][:, 1:], us[:, None]], axis=1))
        last = i == depth - 1
        g_next = norm_final if last else norm_mix[i + 1]
        h_dtype = F32 if last else BF16
        wu = w_up[i].astype(BF16)
        wd = w_down[i].astype(BF16)
        d_ff = wd.shape[0]
        xp, hp, fbp = _ffn_prompt(hp, wu, w_ffn_dw[i], b_ffn_dw[i], wd, xp, g_next, Bp, T,
                                  TM, 256, h_dtype, not last)
        u_s = _mm_plain(hs, wu, d_ff // 2)
        gs = _ffn_gate_sample(u_s, state_ffn[i], w_ffn_dw[i], b_ffn_dw[i])
        xs, hs = _mm_res(gs, wd, None, xs, g_next, Ms, h_dtype, emit_x=not last)
        fp_l.append(fbp)
        fs_l.append(jnp.concatenate([state_ffn[i][:, 1:], u_s[:, None]], axis=1))

    y_prompt = hp.reshape(Bp, T, D)
    y_sample = hs.reshape(Bs, Ts, D)
    return (y_prompt, y_sample, jnp.stack(kp_l), jnp.stack(vp_l), jnp.stack(ks_l), jnp.stack(vs_l),
            jnp.stack(cp_l), jnp.stack(cs_l), jnp.stack(fp_l), jnp.stack(fs_l))
```

```python
import functools
import math

import jax
import jax.numpy as jnp
from jax import lax
from jax.experimental import pallas as pl
from jax.experimental.pallas import tpu as pltpu

F32 = jnp.float32
BF16 = jnp.bfloat16

RMS_EPS = 1e-6
LN_EPS = 1e-5
N_MIXERS = 2
LANES = 128
SUBLANES = 8
VMEM_LIMIT = 56 * 1024 * 1024
ATTN_REDUCE_WAYS = 4
WEIGHT_CAST_ROWS = 256


def _params(*sem):
    return pltpu.CompilerParams(dimension_semantics=sem, vmem_limit_bytes=VMEM_LIMIT)


def _rms_rows(x, g, eps):
    ms = jnp.mean(x * x, axis=-1, keepdims=True)
    return x * lax.rsqrt(ms + eps) * g


def _diff_lambda(lq1, lk1, lq2, lk2, lam_init):
    a = jnp.sum(lq1[...] * lk1[...], keepdims=True)
    b = jnp.sum(lq2[...] * lk2[...], keepdims=True)
    return jnp.exp(a) - jnp.exp(b) + lam_init


def _cast_kernel(w_ref, o_ref):
    o_ref[...] = w_ref[...].astype(o_ref.dtype)


def _layer_bf16(w, i):
    _, R, C = w.shape
    tr = WEIGHT_CAST_ROWS
    return pl.pallas_call(
        _cast_kernel,
        name="cast_bf16",
        grid=(R // tr,),
        in_specs=[pl.BlockSpec((None, tr, C), lambda r: (i, r, 0))],
        out_specs=pl.BlockSpec((tr, C), lambda r: (r, 0)),
        out_shape=jax.ShapeDtypeStruct((R, C), BF16),
        compiler_params=_params("parallel"),
    )(w)


def _qkv_kernel(x_ref, g_ref, w_ref, q_ref, k_ref, v_ref, *, width, scale):
    h = _rms_rows(x_ref[...], g_ref[...], RMS_EPS).astype(BF16)
    q = jnp.dot(h, w_ref[:, 0:width], preferred_element_type=F32)
    q_ref[...] = (q * scale).astype(q_ref.dtype)
    k_ref[...] = jnp.dot(h, w_ref[:, width:2 * width], preferred_element_type=F32)
    v_ref[...] = jnp.dot(h, w_ref[:, 2 * width:3 * width], preferred_element_type=F32)


def _qkv(x, g, w, tm, scale, q_dtype):
    M, D = x.shape
    width = w.shape[1] // 3
    blk = lambda m: (m, 0)
    return pl.pallas_call(
        functools.partial(_qkv_kernel, width=width, scale=scale),
        name="qkv",
        grid=(M // tm,),
        in_specs=[pl.BlockSpec((tm, D), blk),
                  pl.BlockSpec((1, D), lambda m: (0, 0)),
                  pl.BlockSpec(w.shape, lambda m: (0, 0))],
        out_specs=[pl.BlockSpec((tm, width), blk)] * 3,
        out_shape=[jax.ShapeDtypeStruct((M, width), q_dtype),
                   jax.ShapeDtypeStruct((M, width), F32),
                   jax.ShapeDtypeStruct((M, width), F32)],
        compiler_params=_params("parallel"),
    )(x, g.reshape(1, D), w)


def _attn_prompt_kernel(lq1, lk1, lq2, lk2, g_ref, q_ref, k_ref, v_ref, o_ref, kb, vt, acc, sbuf,
                        *, tq, head_dim, lam_init):
    T = q_ref.shape[0]
    vd = 2 * head_dim
    n_heads = q_ref.shape[1] // vd
    nq = T // tq
    streams = [(hh, c) for hh in range(n_heads) for c in range(2)]
    lam = _diff_lambda(lq1, lk1, lq2, lk2, lam_init)
    kb[...] = k_ref[...].astype(BF16)
    for hh in range(n_heads):
        for kj in range(nq):
            vt[hh, kj] = v_ref[kj * tq:(kj + 1) * tq, hh * vd:(hh + 1) * vd].T.astype(BF16)
    first = lax.broadcasted_iota(jnp.int32, (tq, vd), 1) < head_dim
    key_i = lax.broadcasted_iota(jnp.int32, (tq, tq), 0)
    qry_i = lax.broadcasted_iota(jnp.int32, (tq, tq), 1)
    causal = key_i <= qry_i
    g = g_ref[...]

    def reduce_keys(op, x):
        n = x.shape[0] // (SUBLANES * ATTN_REDUCE_WAYS)
        x = op(x.reshape(ATTN_REDUCE_WAYS, n, SUBLANES, x.shape[1]), axis=1)
        return op(op(x, axis=0), axis=0, keepdims=True)

    def scores(i, kj, q_parts):
        start = pl.multiple_of(kj * tq, tq)
        hh = streams[i][0]
        kt = kb[pl.ds(start, tq), hh * vd:(hh + 1) * vd]
        return lax.dot_general(kt, q_parts[i], (((1,), (1,)), ((), ())),
                               preferred_element_type=F32)

    def consume(i, kj, state, s, masked):
        m, l = state
        if masked:
            s = jnp.where(causal, s, -jnp.inf)
        m_new = jnp.maximum(m, reduce_keys(jnp.max, s))
        p = jnp.exp2(s - m_new)
        alpha = jnp.exp2(m - m_new)
        l = alpha * l + reduce_keys(jnp.sum, p)
        acc[i] = alpha * acc[i] + jnp.dot(vt[streams[i][0], kj], p.astype(BF16),
                                          preferred_element_type=F32)
        return m_new, l

    def q_block(qi, _):
        qstart = pl.multiple_of(qi * tq, tq)
        q_parts = []
        for hh, c in streams:
            q = q_ref[pl.ds(qstart, tq), hh * vd:(hh + 1) * vd]
            zero = jnp.zeros_like(q)
            q_parts.append(jnp.where(first, zero, q) if c else jnp.where(first, q, zero))
        acc[...] = jnp.zeros(acc.shape, F32)
        for i in range(len(streams)):
            sbuf[i] = scores(i, 0, q_parts)

        def kv_step(kj, carry):
            cur = [sbuf[i] for i in range(len(streams))]
            for i in range(len(streams)):
                sbuf[i] = scores(i, kj + 1, q_parts)
            return tuple(consume(i, kj, carry[i], cur[i], False) for i in range(len(streams)))

        init = tuple((jnp.full((1, tq), -jnp.inf, F32), jnp.zeros((1, tq), F32)) for _ in streams)
        carry = lax.fori_loop(0, qi, kv_step, init)
        fin = [consume(i, qi, carry[i], sbuf[i], True) for i in range(len(streams))]
        for hh in range(n_heads):
            l0, l1 = fin[2 * hh][1], fin[2 * hh + 1][1]
            o = (acc[2 * hh] / l0 - lam * (acc[2 * hh + 1] / l1)).T
            o = _rms_rows(o, g, RMS_EPS) * (1.0 - lam_init)
            o_ref[pl.ds(qstart, tq), hh * vd:(hh + 1) * vd] = o.astype(o_ref.dtype)
        return 0

    lax.fori_loop(0, nq, q_block, 0)


def _attn_prompt(q, k, v, lams, g, batch, seq, n_heads, head_dim, lam_init, tq, heads_per_step):
    M, width = q.shape
    vd = width // n_heads
    hs = heads_per_step
    blk = pl.BlockSpec((seq, hs * vd), lambda b, h: (b, h))
    small = pl.BlockSpec((1, head_dim), lambda b, h: (0, 0))
    return pl.pallas_call(
        functools.partial(_attn_prompt_kernel, tq=tq, head_dim=head_dim, lam_init=lam_init),
        name="attn_prompt",
        grid=(batch, n_heads // hs),
        in_specs=[small, small, small, small,
                  pl.BlockSpec((1, vd), lambda b, h: (0, 0)), blk, blk, blk],
        out_specs=blk,
        out_shape=jax.ShapeDtypeStruct((M, width), BF16),
        scratch_shapes=[pltpu.VMEM((seq, hs * vd), BF16),
                        pltpu.VMEM((hs, seq // tq, vd, tq), BF16),
                        pltpu.VMEM((2 * hs, vd, tq), F32),
                        pltpu.VMEM((2 * hs, tq, tq), F32)],
        compiler_params=_params("parallel", "parallel"),
    )(*[x.reshape(1, head_dim) for x in lams], g.reshape(1, vd), q, k, v)


def _decode_attn_kernel(pt_ref, lq1, lk1, lq2, lk2, g_ref, q_ref, kn_ref, vn_ref, *refs,
                        n_pages, n_heads, head_dim, lam_init):
    del pt_ref
    k_refs = refs[:n_pages]
    v_refs = refs[n_pages:2 * n_pages]
    o_ref = refs[2 * n_pages]
    m_scr, l_scr, acc_scr = refs[2 * n_pages + 1:]
    c = pl.program_id(1)
    nc = pl.num_programs(1)
    rows_per_page, vd = k_refs[0].shape
    n_rows = 2 * n_heads

    @pl.when(c == 0)
    def _():
        m_scr[...] = jnp.full(m_scr.shape, -jnp.inf, F32)
        l_scr[...] = jnp.zeros(l_scr.shape, F32)
        acc_scr[...] = jnp.zeros(acc_scr.shape, F32)

    q = q_ref[...]
    lane = lax.broadcasted_iota(jnp.int32, q.shape, 1)
    q_rows = jnp.concatenate([jnp.where(lane < head_dim, q, 0.0),
                              jnp.where(lane < head_dim, 0.0, q)], axis=0).astype(BF16)
    assert n_heads & (n_heads - 1) == 0
    row_head = lax.broadcasted_iota(jnp.int32, (n_rows, rows_per_page), 0) & (n_heads - 1)
    col_head = lax.broadcasted_iota(jnp.int32, (n_rows, rows_per_page), 1) & (n_heads - 1)
    same_head = row_head == col_head

    scores = []
    for gi in range(n_pages):
        s = lax.dot_general(q_rows, k_refs[gi][...].astype(BF16), (((1,), (1,)), ((), ())),
                            preferred_element_type=F32)
        scores.append(jnp.where(same_head, s, -jnp.inf))
    m_old = m_scr[...]
    m_new = m_old
    for s in scores:
        m_new = jnp.maximum(m_new, jnp.max(s, axis=-1, keepdims=True))
    alpha = jnp.exp2(m_old - m_new)
    l_new = alpha * l_scr[...]
    acc = alpha * acc_scr[...]
    for gi in range(n_pages):
        p = jnp.exp2(scores[gi] - m_new)
        l_new = l_new + jnp.sum(p, axis=-1, keepdims=True)
        acc = acc + jnp.dot(p.astype(BF16), v_refs[gi][...].astype(BF16),
                            preferred_element_type=F32)
    m_scr[...] = m_new
    l_scr[...] = l_new
    acc_scr[...] = acc

    @pl.when(c == nc - 1)
    def _():
        lam = _diff_lambda(lq1, lk1, lq2, lk2, lam_init)
        kn2 = jnp.concatenate([kn_ref[...]] * 2, axis=0)
        vn2 = jnp.concatenate([vn_ref[...]] * 2, axis=0)
        s_new = jnp.sum(kn2 * q_rows.astype(F32), axis=-1, keepdims=True)
        m_fin = jnp.maximum(m_new, s_new)
        a_fin = jnp.exp2(m_new - m_fin)
        p_new = jnp.exp2(s_new - m_fin)
        l_fin = a_fin * l_new + p_new
        o2 = (a_fin * acc + p_new * vn2) / l_fin
        o = o2[0:n_heads] - lam * o2[n_heads:n_rows]
        o_ref[...] = _rms_rows(o, g_ref[...], RMS_EPS) * (1.0 - lam_init)


def _decode_attn(q, k_new, v_new, cache_k, cache_v, j, page_table, lams, g,
                 n_heads, head_dim, lam_init, pages_per_step):
    B = q.shape[0]
    vd = 2 * head_dim
    n_pages = page_table.shape[1]
    n_layers, n_pool, page = cache_k.shape[:3]
    G = pages_per_step
    small = pl.BlockSpec((1, head_dim), lambda b, c, pt: (0, 0))
    per_b = pl.BlockSpec((None, n_heads, vd), lambda b, c, pt: (b, 0, 0))
    ck = cache_k.reshape(n_layers, n_pool, page * n_heads, vd)
    cv = cache_v.reshape(n_layers, n_pool, page * n_heads, vd)

    def page_spec(gi):
        return pl.BlockSpec((None, None, page * n_heads, vd),
                            lambda b, c, pt: (j, pt[b, c * G + gi], 0, 0))

    grid_spec = pltpu.PrefetchScalarGridSpec(
        num_scalar_prefetch=1,
        grid=(B, n_pages // G),
        in_specs=[small, small, small, small,
                  pl.BlockSpec((1, vd), lambda b, c, pt: (0, 0)),
                  per_b, per_b, per_b]
                 + [page_spec(gi) for gi in range(G)] * 2,
        out_specs=per_b,
        scratch_shapes=[pltpu.VMEM((2 * n_heads, 1), F32), pltpu.VMEM((2 * n_heads, 1), F32),
                        pltpu.VMEM((2 * n_heads, vd), F32)],
    )
    return pl.pallas_call(
        functools.partial(_decode_attn_kernel, n_pages=G, n_heads=n_heads, head_dim=head_dim,
                          lam_init=lam_init),
        name="attn_decode",
        grid_spec=grid_spec,
        out_shape=jax.ShapeDtypeStruct((B, n_heads, vd), F32),
        compiler_params=_params("parallel", "arbitrary"),
    )(page_table, *[x.reshape(1, head_dim) for x in lams], g.reshape(1, vd),
      q.reshape(B, n_heads, vd), k_new.reshape(B, n_heads, vd), v_new.reshape(B, n_heads, vd),
      *([ck] * G), *([cv] * G))


def _mm_res_kernel(*refs, has_bias, emit_x):
    a_ref, w_ref = refs[:2]
    i = 2
    b_ref = None
    if has_bias:
        b_ref = refs[i]
        i += 1
    res_ref, g_ref = refs[i:i + 2]
    outs = refs[i + 2:]
    y = jnp.dot(a_ref[...].astype(BF16), w_ref[...], preferred_element_type=F32)
    if has_bias:
        y = y + b_ref[...]
    x = res_ref[...] + y
    if emit_x:
        outs[0][...] = x
    outs[-1][...] = _rms_rows(x, g_ref[...], RMS_EPS).astype(outs[-1].dtype)


def _mm_res(a, w, bias, res, g, tm, h_dtype, emit_x=True):
    M, K = a.shape
    N = w.shape[1]
    row = lambda m: (m, 0)
    fixed = lambda m: (0, 0)
    ins = [a, w]
    in_specs = [pl.BlockSpec((tm, K), row), pl.BlockSpec((K, N), fixed)]
    if bias is not None:
        ins.append(bias.reshape(1, N))
        in_specs.append(pl.BlockSpec((1, N), fixed))
    ins += [res, g.reshape(1, N)]
    in_specs += [pl.BlockSpec((tm, N), row), pl.BlockSpec((1, N), fixed)]
    out_specs = [pl.BlockSpec((tm, N), row)]
    out_shape = [jax.ShapeDtypeStruct((M, N), h_dtype)]
    if emit_x:
        out_specs = [pl.BlockSpec((tm, N), row)] + out_specs
        out_shape = [jax.ShapeDtypeStruct((M, N), F32)] + out_shape
    out = pl.pallas_call(
        functools.partial(_mm_res_kernel, has_bias=bias is not None, emit_x=emit_x),
        name="mm_res",
        grid=(M // tm,),
        in_specs=in_specs, out_specs=out_specs, out_shape=out_shape,
        compiler_params=_params("parallel"),
    )(*ins)
    return out if emit_x else (None, out[0])


def _glu_kernel(a_ref, w_ref, b_ref, u_ref, *, ch):
    a = a_ref[...]
    lin = jnp.dot(a, w_ref[:, 0:ch], preferred_element_type=F32) + b_ref[:, 0:ch]
    gate = jnp.dot(a, w_ref[:, ch:2 * ch], preferred_element_type=F32) + b_ref[:, ch:2 * ch]
    u_ref[...] = lin * jax.nn.sigmoid(gate)


def _glu(a, w, b, tm):
    M, K = a.shape
    ch = w.shape[1] // 2
    return pl.pallas_call(
        functools.partial(_glu_kernel, ch=ch),
        name="glu",
        grid=(M // tm,),
        in_specs=[pl.BlockSpec((tm, K), lambda m: (m, 0)),
                  pl.BlockSpec(w.shape, lambda m: (0, 0)),
                  pl.BlockSpec((1, 2 * ch), lambda m: (0, 0))],
        out_specs=pl.BlockSpec((tm, ch), lambda m: (m, 0)),
        out_shape=jax.ShapeDtypeStruct((M, ch), F32),
        compiler_params=_params("parallel"),
    )(a, w, b.reshape(1, 2 * ch))


def _conv_prompt_kernel(u_ref, halo_ref, wdw_ref, bdw_ref, lng_ref, lnb_ref, w_ref, b_ref,
                        res_ref, g_ref, x_ref, h_ref, ubuf, cbuf, wb, *, tiles_per_seq, halo, chunk):
    m = pl.program_id(0)
    tm, ch = u_ref.shape
    width = wdw_ref.shape[0]
    seq_start = (m % tiles_per_seq) == 0
    ubuf[0, 0:halo, :] = jnp.where(seq_start, 0.0, halo_ref[...])
    ubuf[0, halo:halo + tm, :] = u_ref[...]
    n_rows = halo + tm
    base = ubuf[0]
    for r in range(1, SUBLANES):
        ubuf[r] = pltpu.roll(base, n_rows - r, axis=0)
    off = halo - (width - 1)
    wb[...] = jnp.broadcast_to(wdw_ref[...][:, None, :], wb.shape)
    bdw = bdw_ref[...]
    groups = chunk // SUBLANES
    for c0 in range(0, tm, chunk):
        acc = jnp.broadcast_to(bdw, (groups, SUBLANES, ch))
        for j in range(width):
            a, r = divmod(off + j, SUBLANES)
            win = ubuf[r, c0 + a * SUBLANES:c0 + a * SUBLANES + chunk, :]
            acc = acc + wb[j][None] * win.reshape(groups, SUBLANES, ch)
        cbuf[c0:c0 + chunk, :] = acc.reshape(chunk, ch)
    c = cbuf[...]
    mu = jnp.mean(c, axis=-1, keepdims=True)
    xc = c - mu
    y = xc * lax.rsqrt(jnp.mean(xc * xc, axis=-1, keepdims=True) + LN_EPS)
    y = y * lng_ref[...] + lnb_ref[...]
    y = y * jax.nn.sigmoid(y)
    out = jnp.dot(y.astype(BF16), w_ref[...], preferred_element_type=F32) + b_ref[...]
    x = res_ref[...] + out
    x_ref[...] = x
    h_ref[...] = _rms_rows(x, g_ref[...], RMS_EPS).astype(h_ref.dtype)


def _conv_prompt(u, w_dw, b_dw, ln_g, ln_b, w2, b2, res, g, seq, tm):
    M, ch = u.shape
    width = w_dw.shape[0]
    halo = -(-(width - 1) // SUBLANES) * SUBLANES
    row = lambda m: (m, 0)
    fixed = lambda m: (0, 0)
    vec = pl.BlockSpec((1, ch), fixed)
    return pl.pallas_call(
        functools.partial(_conv_prompt_kernel, tiles_per_seq=seq // tm, halo=halo, chunk=32),
        name="conv_prompt",
        grid=(M // tm,),
        in_specs=[pl.BlockSpec((tm, ch), row),
                  pl.BlockSpec((halo, ch), lambda m: (jnp.maximum(m * (tm // halo) - 1, 0), 0)),
                  pl.BlockSpec((width, ch), fixed), vec, vec, vec,
                  pl.BlockSpec(w2.shape, fixed), vec,
                  pl.BlockSpec((tm, ch), row), vec],
        out_specs=[pl.BlockSpec((tm, ch), row)] * 2,
        out_shape=[jax.ShapeDtypeStruct((M, ch), F32), jax.ShapeDtypeStruct((M, ch), BF16)],
        scratch_shapes=[pltpu.VMEM((SUBLANES, halo + tm, ch), F32), pltpu.VMEM((tm, ch), F32),
                        pltpu.VMEM((width, SUBLANES, ch), F32)],
        compiler_params=_params("parallel"),
    )(u, u, w_dw, b_dw.reshape(1, ch), ln_g.reshape(1, ch), ln_b.reshape(1, ch), w2,
      b2.reshape(1, ch), res, g.reshape(1, ch))


def _ffn_prompt_kernel(h_ref, wu_ref, wdw_ref, bdw_ref, wd_ref, res_ref, g_ref,
                       *refs, tiles_per_seq, emit_x, sub):
    if emit_x:
        x_ref, hn_ref, st_ref, carry, gbuf = refs
    else:
        hn_ref, st_ref, carry, gbuf = refs
    m = pl.program_id(0)
    tm = h_ref.shape[0]
    d_ff = wd_ref.shape[0]
    pad = SUBLANES
    h = h_ref[...]
    seq_start = (m % tiles_per_seq) == 0
    for c0 in range(0, d_ff, sub):
        conv = []
        for half in range(2):
            cols = slice(half * d_ff + c0, half * d_ff + c0 + sub)
            u = jnp.dot(h, wu_ref[:, cols], preferred_element_type=F32)
            prev = jnp.where(seq_start, 0.0, carry[:, cols])
            carry[:, cols] = u[tm - pad:, :]
            st_ref[0, half, :, c0:c0 + sub] = u[tm - 2:, :]
            ext = jnp.concatenate([prev, u], axis=0)
            conv.append(wdw_ref[2:3, cols] * u + wdw_ref[1:2, cols] * ext[pad - 1:pad - 1 + tm]
                        + wdw_ref[0:1, cols] * ext[pad - 2:pad - 2 + tm] + bdw_ref[:, cols])
        gbuf[:, c0:c0 + sub] = ((conv[0] * jax.nn.sigmoid(conv[0])) * conv[1]).astype(BF16)
    y = jnp.dot(gbuf[...], wd_ref[...], preferred_element_type=F32)
    x = res_ref[...] + y
    if emit_x:
        x_ref[...] = x
    hn_ref[...] = _rms_rows(x, g_ref[...], RMS_EPS).astype(hn_ref.dtype)


def _ffn_prompt(h, w_up, w_dw, b_dw, w_down, res, g, batch, seq, tm, sub, h_dtype, emit_x):
    M, D = h.shape
    d_ff = w_down.shape[0]
    tiles_per_seq = seq // tm
    row = lambda m: (m, 0)
    fixed = lambda m: (0, 0)
    once = dict(pipeline_mode=pl.Buffered(1))
    out_specs = [pl.BlockSpec((tm, D), row),
                 pl.BlockSpec((1, 2, 2, d_ff), lambda m: (m, 0, 0, 0))]
    out_shape = [jax.ShapeDtypeStruct((M, D), h_dtype),
                 jax.ShapeDtypeStruct((M // tm, 2, 2, d_ff), F32)]
    if emit_x:
        out_specs = [pl.BlockSpec((tm, D), row)] + out_specs
        out_shape = [jax.ShapeDtypeStruct((M, D), F32)] + out_shape
    out = pl.pallas_call(
        functools.partial(_ffn_prompt_kernel, tiles_per_seq=tiles_per_seq, emit_x=emit_x, sub=sub),
        name="ffn_prompt",
        grid=(M // tm,),
        in_specs=[pl.BlockSpec((tm, D), row),
                  pl.BlockSpec((D, 2 * d_ff), fixed, **once),
                  pl.BlockSpec((3, 2 * d_ff), fixed, **once),
                  pl.BlockSpec((1, 2 * d_ff), fixed, **once),
                  pl.BlockSpec((d_ff, D), fixed, **once),
                  pl.BlockSpec((tm, D), row),
                  pl.BlockSpec((1, D), fixed, **once)],
        out_specs=out_specs, out_shape=out_shape,
        scratch_shapes=[pltpu.VMEM((SUBLANES, 2 * d_ff), F32), pltpu.VMEM((tm, d_ff), BF16)],
        compiler_params=_params("arbitrary"),
    )(h, w_up, w_dw, b_dw.reshape(1, 2 * d_ff), w_down, res, g.reshape(1, D))
    if emit_x:
        x, hn, st = out
    else:
        (hn, st), x = out, None
    st = st[tiles_per_seq - 1::tiles_per_seq].transpose(0, 2, 1, 3).reshape(batch, 2, 2 * d_ff)
    return x, hn, st


def _mm_plain_kernel(a_ref, w_ref, o_ref):
    o_ref[...] = jnp.dot(a_ref[...], w_ref[...], preferred_element_type=F32)


def _mm_plain(a, w, tn):
    M, K = a.shape
    N = w.shape[1]
    return pl.pallas_call(
        _mm_plain_kernel,
        name="mm_plain",
        grid=(N // tn,),
        in_specs=[pl.BlockSpec((M, K), lambda n: (0, 0)), pl.BlockSpec((K, tn), lambda n: (0, n))],
        out_specs=pl.BlockSpec((M, tn), lambda n: (0, n)),
        out_shape=jax.ShapeDtypeStruct((M, N), F32),
        compiler_params=_params("parallel"),
    )(a, w)


def _ffn_gate_sample_kernel(u_ref, s0_ref, s1_ref, w_ref, b_ref, o_ref, *, d_ff):
    cv = (w_ref[2:3, :] * u_ref[...] + w_ref[1:2, :] * s1_ref[...]
          + w_ref[0:1, :] * s0_ref[...] + b_ref[...])
    a = cv[:, 0:d_ff]
    o_ref[...] = ((a * jax.nn.sigmoid(a)) * cv[:, d_ff:2 * d_ff]).astype(o_ref.dtype)


def _ffn_gate_sample(u, state, w_dw, b_dw):
    B, two_ff = u.shape
    d_ff = two_ff // 2
    full = lambda s: pl.BlockSpec(s, lambda i: (0, 0))
    return pl.pallas_call(
        functools.partial(_ffn_gate_sample_kernel, d_ff=d_ff),
        name="ffn_gate_sample",
        grid=(1,),
        in_specs=[full((B, two_ff))] * 3 + [full((3, two_ff)), full((1, two_ff))],
        out_specs=full((B, d_ff)),
        out_shape=jax.ShapeDtypeStruct((B, d_ff), BF16),
        compiler_params=_params("arbitrary"),
    )(u, state[:, 0], state[:, 1], w_dw, b_dw.reshape(1, two_ff))


def _conv_sample_kernel(u_ref, st_ref, wdw_ref, bdw_ref, lng_ref, lnb_ref, o_ref):
    width = wdw_ref.shape[0]
    c = jnp.sum(st_ref[...] * wdw_ref[0:width - 1, :][None], axis=1)
    c = c + u_ref[...] * wdw_ref[width - 1:width, :] + bdw_ref[...]
    mu = jnp.mean(c, axis=-1, keepdims=True)
    xc = c - mu
    y = xc * lax.rsqrt(jnp.mean(xc * xc, axis=-1, keepdims=True) + LN_EPS)
    y = y * lng_ref[...] + lnb_ref[...]
    o_ref[...] = (y * jax.nn.sigmoid(y)).astype(o_ref.dtype)


def _conv_sample(u, state, w_dw, b_dw, ln_g, ln_b):
    B, ch = u.shape
    full2 = lambda s: pl.BlockSpec(s, lambda i: (0, 0))
    vec = full2((1, ch))
    return pl.pallas_call(
        _conv_sample_kernel,
        name="conv_sample",
        grid=(1,),
        in_specs=[full2((B, ch)), pl.BlockSpec(state.shape, lambda i: (0, 0, 0)),
                  full2(w_dw.shape), vec, vec, vec],
        out_specs=full2((B, ch)),
        out_shape=jax.ShapeDtypeStruct((B, ch), BF16),
        compiler_params=_params("arbitrary"),
    )(u, state, w_dw, b_dw.reshape(1, ch), ln_g.reshape(1, ch), ln_b.reshape(1, ch))


def kernel(x_prompt, x_sample, cache_k, cache_v, page_table, state_conv, state_ffn, norm_mix, norm_ffn, norm_final, w_qkv, w_o, lambda_q1, lambda_k1, lambda_q2, lambda_k2, subln_g, w_pw1, b_pw1, w_dw, b_dw, ln_g, ln_b, w_pw2, b_pw2, w_up, w_ffn_dw, b_ffn_dw, w_down):
    Bp, T, D = x_prompt.shape
    Bs, Ts, _ = x_sample.shape
    assert Ts == 1, "sample group is one token per sequence"
    depth = norm_mix.shape[0]
    head_dim = lambda_q1.shape[1]
    vd = subln_g.shape[1]
    n_heads = w_o.shape[1] // vd
    scale = head_dim ** -0.5 * math.log2(math.e)
    Mp, Ms = Bp * T, Bs * Ts
    TM = 512

    xp = x_prompt.reshape(Mp, D)
    xs = x_sample.reshape(Ms, D)
    hp = hs = None

    kp_l, vp_l, ks_l, vs_l = [], [], [], []
    cp_l, cs_l, fp_l, fs_l = [], [], [], []
    for i in range(depth):
        j = i // N_MIXERS
        if i % N_MIXERS == 0:
            lam_init = 0.8 - 0.6 * math.exp(-0.3 * i)
            lams = (lambda_q1[j], lambda_k1[j], lambda_q2[j], lambda_k2[j])
            wq = _layer_bf16(w_qkv, j)
            wo = _layer_bf16(w_o, j)
            qp, kp, vp = _qkv(xp, norm_mix[i], wq, TM, scale, BF16)
            qs, ks, vs = _qkv(xs, norm_mix[i], wq, Ms, scale, F32)
            op = _attn_prompt(qp, kp, vp, lams, subln_g[j], Bp, T, n_heads, head_dim, lam_init, 512, 2)
            os_ = _decode_attn(qs, ks, vs, cache_k, cache_v, j, page_table, lams, subln_g[j],
                               n_heads, head_dim, lam_init, 8)
            xp, hp = _mm_res(op, wo, None, xp, norm_ffn[i], TM, BF16)
            xs, hs = _mm_res(os_.reshape(Ms, n_heads * vd), wo, None, xs, norm_ffn[i], Ms, BF16)
            kp_l.append(kp.reshape(Bp, T, n_heads, vd))
            vp_l.append(vp.reshape(Bp, T, n_heads, vd))
            ks_l.append(ks.reshape(Bs, Ts, n_heads, vd))
            vs_l.append(vs.reshape(Bs, Ts, n_heads, vd))
        else:
            w1 = _layer_bf16(w_pw1, j)
            w2 = _layer_bf16(w_pw2, j)
            width = w_dw.shape[1]
            up = _glu(hp, w1, b_pw1[j], TM)
            us = _glu(hs, w1, b_pw1[j], Ms)
            cs = _conv_sample(us, state_conv[j], w_dw[j], b_dw[j], ln_g[j], ln_b[j])
            xp, hp = _conv_prompt(up, w_dw[j], b_dw[j], ln_g[j], ln_b[j], w2, b_pw2[j], xp,
                                  norm_ffn[i], T, 256)
            xs, hs = _mm_res(cs, w2, b_pw2[j], xs, norm_ffn[i], Ms, BF16)
            cp_l.append(up.reshape(Bp, T, -1)[:, T - (width - 1):])
            cs_l.append(jnp.concatenate([state_conv[j][:, 1:], us[:, None]], axis=1))
        last = i == depth - 1
        g_next = norm_final if last else norm_mix[i + 1]
        h_dtype = F32 if last else BF16
        wu = _layer_bf16(w_up, i)
        wd = _layer_bf16(w_down, i)
        d_ff = wd.shape[0]
        xp, hp, fbp = _ffn_prompt(hp, wu, w_ffn_dw[i], b_ffn_dw[i], wd, xp, g_next, Bp, T,
                                  TM, 256, h_dtype, not last)
        u_s = _mm_plain(hs, wu, d_ff // 2)
        gs = _ffn_gate_sample(u_s, state_ffn[i], w_ffn_dw[i], b_ffn_dw[i])
        xs, hs = _mm_res(gs, wd, None, xs, g_next, Ms, h_dtype, emit_x=not last)
        fp_l.append(fbp)
        fs_l.append(jnp.concatenate([state_ffn[i][:, 1:], u_s[:, None]], axis=1))

    y_prompt = hp.reshape(Bp, T, D)
    y_sample = hs.reshape(Bs, Ts, D)
    return (y_prompt, y_sample, jnp.stack(kp_l), jnp.stack(vp_l), jnp.stack(ks_l), jnp.stack(vs_l),
            jnp.stack(cp_l), jnp.stack(cs_l), jnp.stack(fp_l), jnp.stack(fs_l))
```

```python
import functools
import math

import jax
import jax.numpy as jnp
from jax import lax
from jax.experimental import pallas as pl
from jax.experimental.pallas import tpu as pltpu

F32 = jnp.float32
BF16 = jnp.bfloat16

RMS_EPS = 1e-6
LN_EPS = 1e-5
N_MIXERS = 2
LANES = 128
SUBLANES = 8
VMEM_LIMIT = 56 * 1024 * 1024
ATTN_REDUCE_WAYS = 4
WEIGHT_CAST_STEPS = 16


def _params(*sem):
    return pltpu.CompilerParams(dimension_semantics=sem, vmem_limit_bytes=VMEM_LIMIT)


def _rms_rows(x, g, eps):
    ms = jnp.mean(x * x, axis=-1, keepdims=True)
    return x * lax.rsqrt(ms + eps) * g


def _diff_lambda(lq1, lk1, lq2, lk2, lam_init):
    a = jnp.sum(lq1[...] * lk1[...], keepdims=True)
    b = jnp.sum(lq2[...] * lk2[...], keepdims=True)
    return jnp.exp(a) - jnp.exp(b) + lam_init


def _cast_kernel(w_ref, o_ref):
    o_ref[...] = w_ref[...].astype(o_ref.dtype)


def _layer_bf16(w, i):
    _, R, C = w.shape
    tr = R // WEIGHT_CAST_STEPS
    assert tr % SUBLANES == 0 and tr * WEIGHT_CAST_STEPS == R
    return pl.pallas_call(
        _cast_kernel,
        name="cast_bf16",
        grid=(R // tr,),
        in_specs=[pl.BlockSpec((None, tr, C), lambda r: (i, r, 0))],
        out_specs=pl.BlockSpec((tr, C), lambda r: (r, 0)),
        out_shape=jax.ShapeDtypeStruct((R, C), BF16),
        compiler_params=_params("parallel"),
    )(w)


def _qkv_kernel(x_ref, g_ref, w_ref, q_ref, k_ref, v_ref, *, width, scale):
    h = _rms_rows(x_ref[...], g_ref[...], RMS_EPS).astype(BF16)
    q = jnp.dot(h, w_ref[:, 0:width], preferred_element_type=F32)
    q_ref[...] = (q * scale).astype(q_ref.dtype)
    k_ref[...] = jnp.dot(h, w_ref[:, width:2 * width], preferred_element_type=F32)
    v_ref[...] = jnp.dot(h, w_ref[:, 2 * width:3 * width], preferred_element_type=F32)


def _qkv(x, g, w, tm, scale, q_dtype):
    M, D = x.shape
    width = w.shape[1] // 3
    blk = lambda m: (m, 0)
    return pl.pallas_call(
        functools.partial(_qkv_kernel, width=width, scale=scale),
        name="qkv",
        grid=(M // tm,),
        in_specs=[pl.BlockSpec((tm, D), blk),
                  pl.BlockSpec((1, D), lambda m: (0, 0)),
                  pl.BlockSpec(w.shape, lambda m: (0, 0))],
        out_specs=[pl.BlockSpec((tm, width), blk)] * 3,
        out_shape=[jax.ShapeDtypeStruct((M, width), q_dtype),
                   jax.ShapeDtypeStruct((M, width), F32),
                   jax.ShapeDtypeStruct((M, width), F32)],
        compiler_params=_params("parallel"),
    )(x, g.reshape(1, D), w)


def _attn_prompt_kernel(lq1, lk1, lq2, lk2, g_ref, q_ref, k_ref, v_ref, o_ref, kb, vt, acc, sbuf,
                        *, tq, head_dim, lam_init):
    T = q_ref.shape[0]
    vd = 2 * head_dim
    n_heads = q_ref.shape[1] // vd
    nq = T // tq
    streams = [(hh, c) for hh in range(n_heads) for c in range(2)]
    lam = _diff_lambda(lq1, lk1, lq2, lk2, lam_init)
    kb[...] = k_ref[...].astype(BF16)
    for hh in range(n_heads):
        for kj in range(nq):
            vt[hh, kj] = v_ref[kj * tq:(kj + 1) * tq, hh * vd:(hh + 1) * vd].T.astype(BF16)
    first = lax.broadcasted_iota(jnp.int32, (tq, vd), 1) < head_dim
    key_i = lax.broadcasted_iota(jnp.int32, (tq, tq), 0)
    qry_i = lax.broadcasted_iota(jnp.int32, (tq, tq), 1)
    causal = key_i <= qry_i
    g = g_ref[...]

    def reduce_keys(op, x):
        n = x.shape[0] // (SUBLANES * ATTN_REDUCE_WAYS)
        x = op(x.reshape(ATTN_REDUCE_WAYS, n, SUBLANES, x.shape[1]), axis=1)
        return op(op(x, axis=0), axis=0, keepdims=True)

    def scores(i, kj, q_parts):
        start = kj * tq if isinstance(kj, int) else pl.multiple_of(kj * tq, tq)
        hh = streams[i][0]
        kt = kb[pl.ds(start, tq), hh * vd:(hh + 1) * vd]
        return lax.dot_general(kt, q_parts[i], (((1,), (1,)), ((), ())),
                               preferred_element_type=F32)

    def consume(i, kj, state, s, masked):
        m, l = state
        if masked:
            s = jnp.where(causal, s, -jnp.inf)
        m_new = jnp.maximum(m, reduce_keys(jnp.max, s))
        p = jnp.exp2(s - m_new)
        alpha = jnp.exp2(m - m_new)
        l = alpha * l + reduce_keys(jnp.sum, p)
        acc[i] = alpha * acc[i] + jnp.dot(vt[streams[i][0], kj], p.astype(BF16),
                                          preferred_element_type=F32)
        return m_new, l

    def query_parts(qi):
        qstart = qi * tq if isinstance(qi, int) else pl.multiple_of(qi * tq, tq)
        parts = []
        for hh, c in streams:
            q = q_ref[pl.ds(qstart, tq), hh * vd:(hh + 1) * vd]
            zero = jnp.zeros_like(q)
            parts.append(jnp.where(first, zero, q) if c else jnp.where(first, q, zero))
        return parts

    def fill_scores(kj, q_parts):
        for i in range(len(streams)):
            sbuf[i] = scores(i, kj, q_parts)

    def q_block(qi, _):
        qstart = pl.multiple_of(qi * tq, tq)
        q_parts = query_parts(qi)
        acc[...] = jnp.zeros(acc.shape, F32)

        def kv_step(kj, carry):
            cur = [sbuf[i] for i in range(len(streams))]
            fill_scores(kj + 1, q_parts)
            return tuple(consume(i, kj, carry[i], cur[i], False) for i in range(len(streams)))

        init = tuple((jnp.full((1, tq), -jnp.inf, F32), jnp.zeros((1, tq), F32)) for _ in streams)
        carry = lax.fori_loop(0, qi, kv_step, init)
        diag = [sbuf[i] for i in range(len(streams))]
        fill_scores(0, query_parts(jnp.minimum(qi + 1, nq - 1)))
        fin = [consume(i, qi, carry[i], diag[i], True) for i in range(len(streams))]
        for hh in range(n_heads):
            l0, l1 = fin[2 * hh][1], fin[2 * hh + 1][1]
            o = (acc[2 * hh] / l0 - lam * (acc[2 * hh + 1] / l1)).T
            o = _rms_rows(o, g, RMS_EPS) * (1.0 - lam_init)
            o_ref[pl.ds(qstart, tq), hh * vd:(hh + 1) * vd] = o.astype(o_ref.dtype)
        return 0

    fill_scores(0, query_parts(0))
    lax.fori_loop(0, nq, q_block, 0)


def _attn_prompt(q, k, v, lams, g, batch, seq, n_heads, head_dim, lam_init, tq, heads_per_step):
    M, width = q.shape
    vd = width // n_heads
    hs = heads_per_step
    blk = pl.BlockSpec((seq, hs * vd), lambda b, h: (b, h))
    small = pl.BlockSpec((1, head_dim), lambda b, h: (0, 0))
    return pl.pallas_call(
        functools.partial(_attn_prompt_kernel, tq=tq, head_dim=head_dim, lam_init=lam_init),
        name="attn_prompt",
        grid=(batch, n_heads // hs),
        in_specs=[small, small, small, small,
                  pl.BlockSpec((1, vd), lambda b, h: (0, 0)), blk, blk, blk],
        out_specs=blk,
        out_shape=jax.ShapeDtypeStruct((M, width), BF16),
        scratch_shapes=[pltpu.VMEM((seq, hs * vd), BF16),
                        pltpu.VMEM((hs, seq // tq, vd, tq), BF16),
                        pltpu.VMEM((2 * hs, vd, tq), F32),
                        pltpu.VMEM((2 * hs, tq, tq), F32)],
        compiler_params=_params("parallel", "parallel"),
    )(*[x.reshape(1, head_dim) for x in lams], g.reshape(1, vd), q, k, v)


def _decode_attn_kernel(pt_ref, lq1, lk1, lq2, lk2, g_ref, q_ref, kn_ref, vn_ref, *refs,
                        n_pages, n_heads, head_dim, lam_init):
    del pt_ref
    k_refs = refs[:n_pages]
    v_refs = refs[n_pages:2 * n_pages]
    o_ref = refs[2 * n_pages]
    m_scr, l_scr, acc_scr = refs[2 * n_pages + 1:]
    c = pl.program_id(1)
    nc = pl.num_programs(1)
    rows_per_page, vd = k_refs[0].shape
    n_rows = 2 * n_heads

    @pl.when(c == 0)
    def _():
        m_scr[...] = jnp.full(m_scr.shape, -jnp.inf, F32)
        l_scr[...] = jnp.zeros(l_scr.shape, F32)
        acc_scr[...] = jnp.zeros(acc_scr.shape, F32)

    q = q_ref[...]
    lane = lax.broadcasted_iota(jnp.int32, q.shape, 1)
    q_rows = jnp.concatenate([jnp.where(lane < head_dim, q, 0.0),
                              jnp.where(lane < head_dim, 0.0, q)], axis=0).astype(BF16)
    assert n_heads & (n_heads - 1) == 0
    row_head = lax.broadcasted_iota(jnp.int32, (n_rows, rows_per_page), 0) & (n_heads - 1)
    col_head = lax.broadcasted_iota(jnp.int32, (n_rows, rows_per_page), 1) & (n_heads - 1)
    same_head = row_head == col_head

    scores = []
    for gi in range(n_pages):
        s = lax.dot_general(q_rows, k_refs[gi][...].astype(BF16), (((1,), (1,)), ((), ())),
                            preferred_element_type=F32)
        scores.append(jnp.where(same_head, s, -jnp.inf))
    m_old = m_scr[...]
    m_new = m_old
    for s in scores:
        m_new = jnp.maximum(m_new, jnp.max(s, axis=-1, keepdims=True))
    alpha = jnp.exp2(m_old - m_new)
    l_new = alpha * l_scr[...]
    acc = alpha * acc_scr[...]
    for gi in range(n_pages):
        p = jnp.exp2(scores[gi] - m_new)
        l_new = l_new + jnp.sum(p, axis=-1, keepdims=True)
        acc = acc + jnp.dot(p.astype(BF16), v_refs[gi][...].astype(BF16),
                            preferred_element_type=F32)
    m_scr[...] = m_new
    l_scr[...] = l_new
    acc_scr[...] = acc

    @pl.when(c == nc - 1)
    def _():
        lam = _diff_lambda(lq1, lk1, lq2, lk2, lam_init)
        kn2 = jnp.concatenate([kn_ref[...]] * 2, axis=0)
        vn2 = jnp.concatenate([vn_ref[...]] * 2, axis=0)
        s_new = jnp.sum(kn2 * q_rows.astype(F32), axis=-1, keepdims=True)
        m_fin = jnp.maximum(m_new, s_new)
        a_fin = jnp.exp2(m_new - m_fin)
        p_new = jnp.exp2(s_new - m_fin)
        l_fin = a_fin * l_new + p_new
        o2 = (a_fin * acc + p_new * vn2) / l_fin
        o = o2[0:n_heads] - lam * o2[n_heads:n_rows]
        o_ref[...] = _rms_rows(o, g_ref[...], RMS_EPS) * (1.0 - lam_init)


def _decode_attn(q, k_new, v_new, cache_k, cache_v, j, page_table, lams, g,
                 n_heads, head_dim, lam_init, pages_per_step):
    B = q.shape[0]
    vd = 2 * head_dim
    n_pages = page_table.shape[1]
    n_layers, n_pool, page = cache_k.shape[:3]
    G = pages_per_step
    small = pl.BlockSpec((1, head_dim), lambda b, c, pt: (0, 0))
    per_b = pl.BlockSpec((None, n_heads, vd), lambda b, c, pt: (b, 0, 0))
    ck = cache_k.reshape(n_layers, n_pool, page * n_heads, vd)
    cv = cache_v.reshape(n_layers, n_pool, page * n_heads, vd)

    def page_spec(gi):
        return pl.BlockSpec((None, None, page * n_heads, vd),
                            lambda b, c, pt: (j, pt[b, c * G + gi], 0, 0))

    grid_spec = pltpu.PrefetchScalarGridSpec(
        num_scalar_prefetch=1,
        grid=(B, n_pages // G),
        in_specs=[small, small, small, small,
                  pl.BlockSpec((1, vd), lambda b, c, pt: (0, 0)),
                  per_b, per_b, per_b]
                 + [page_spec(gi) for gi in range(G)] * 2,
        out_specs=per_b,
        scratch_shapes=[pltpu.VMEM((2 * n_heads, 1), F32), pltpu.VMEM((2 * n_heads, 1), F32),
                        pltpu.VMEM((2 * n_heads, vd), F32)],
    )
    return pl.pallas_call(
        functools.partial(_decode_attn_kernel, n_pages=G, n_heads=n_heads, head_dim=head_dim,
                          lam_init=lam_init),
        name="attn_decode",
        grid_spec=grid_spec,
        out_shape=jax.ShapeDtypeStruct((B, n_heads, vd), F32),
        compiler_params=_params("parallel", "arbitrary"),
    )(page_table, *[x.reshape(1, head_dim) for x in lams], g.reshape(1, vd),
      q.reshape(B, n_heads, vd), k_new.reshape(B, n_heads, vd), v_new.reshape(B, n_heads, vd),
      *([ck] * G), *([cv] * G))


def _mm_res_kernel(*refs, has_bias, emit_x):
    a_ref, w_ref = refs[:2]
    i = 2
    b_ref = None
    if has_bias:
        b_ref = refs[i]
        i += 1
    res_ref, g_ref = refs[i:i + 2]
    outs = refs[i + 2:]
    y = jnp.dot(a_ref[...].astype(BF16), w_ref[...], preferred_element_type=F32)
    if has_bias:
        y = y + b_ref[...]
    x = res_ref[...] + y
    if emit_x:
        outs[0][...] = x
    outs[-1][...] = _rms_rows(x, g_ref[...], RMS_EPS).astype(outs[-1].dtype)


def _mm_res(a, w, bias, res, g, tm, h_dtype, emit_x=True):
    M, K = a.shape
    N = w.shape[1]
    row = lambda m: (m, 0)
    fixed = lambda m: (0, 0)
    ins = [a, w]
    in_specs = [pl.BlockSpec((tm, K), row), pl.BlockSpec((K, N), fixed)]
    if bias is not None:
        ins.append(bias.reshape(1, N))
        in_specs.append(pl.BlockSpec((1, N), fixed))
    ins += [res, g.reshape(1, N)]
    in_specs += [pl.BlockSpec((tm, N), row), pl.BlockSpec((1, N), fixed)]
    out_specs = [pl.BlockSpec((tm, N), row)]
    out_shape = [jax.ShapeDtypeStruct((M, N), h_dtype)]
    if emit_x:
        out_specs = [pl.BlockSpec((tm, N), row)] + out_specs
        out_shape = [jax.ShapeDtypeStruct((M, N), F32)] + out_shape
    out = pl.pallas_call(
        functools.partial(_mm_res_kernel, has_bias=bias is not None, emit_x=emit_x),
        name="mm_res",
        grid=(M // tm,),
        in_specs=in_specs, out_specs=out_specs, out_shape=out_shape,
        compiler_params=_params("parallel"),
    )(*ins)
    return out if emit_x else (None, out[0])


def _glu_kernel(a_ref, w_ref, b_ref, u_ref, *, ch):
    a = a_ref[...]
    lin = jnp.dot(a, w_ref[:, 0:ch], preferred_element_type=F32) + b_ref[:, 0:ch]
    gate = jnp.dot(a, w_ref[:, ch:2 * ch], preferred_element_type=F32) + b_ref[:, ch:2 * ch]
    u_ref[...] = lin * jax.nn.sigmoid(gate)


def _glu(a, w, b, tm):
    M, K = a.shape
    ch = w.shape[1] // 2
    return pl.pallas_call(
        functools.partial(_glu_kernel, ch=ch),
        name="glu",
        grid=(M // tm,),
        in_specs=[pl.BlockSpec((tm, K), lambda m: (m, 0)),
                  pl.BlockSpec(w.shape, lambda m: (0, 0)),
                  pl.BlockSpec((1, 2 * ch), lambda m: (0, 0))],
        out_specs=pl.BlockSpec((tm, ch), lambda m: (m, 0)),
        out_shape=jax.ShapeDtypeStruct((M, ch), F32),
        compiler_params=_params("parallel"),
    )(a, w, b.reshape(1, 2 * ch))


def _conv_prompt_kernel(u_ref, halo_ref, wdw_ref, bdw_ref, lng_ref, lnb_ref, w_ref, b_ref,
                        res_ref, g_ref, x_ref, h_ref, ubuf, cbuf, wb, *, tiles_per_seq, halo, chunk):
    m = pl.program_id(0)
    tm, ch = u_ref.shape
    width = wdw_ref.shape[0]
    seq_start = (m % tiles_per_seq) == 0
    ubuf[0, 0:halo, :] = jnp.where(seq_start, 0.0, halo_ref[...])
    ubuf[0, halo:halo + tm, :] = u_ref[...]
    n_rows = halo + tm
    base = ubuf[0]
    for r in range(1, SUBLANES):
        ubuf[r] = pltpu.roll(base, n_rows - r, axis=0)
    off = halo - (width - 1)
    wb[...] = jnp.broadcast_to(wdw_ref[...][:, None, :], wb.shape)
    bdw = bdw_ref[...]
    groups = chunk // SUBLANES
    for c0 in range(0, tm, chunk):
        acc = jnp.broadcast_to(bdw, (groups, SUBLANES, ch))
        for j in range(width):
            a, r = divmod(off + j, SUBLANES)
            win = ubuf[r, c0 + a * SUBLANES:c0 + a * SUBLANES + chunk, :]
            acc = acc + wb[j][None] * win.reshape(groups, SUBLANES, ch)
        cbuf[c0:c0 + chunk, :] = acc.reshape(chunk, ch)
    c = cbuf[...]
    mu = jnp.mean(c, axis=-1, keepdims=True)
    xc = c - mu
    y = xc * lax.rsqrt(jnp.mean(xc * xc, axis=-1, keepdims=True) + LN_EPS)
    y = y * lng_ref[...] + lnb_ref[...]
    y = y * jax.nn.sigmoid(y)
    out = jnp.dot(y.astype(BF16), w_ref[...], preferred_element_type=F32) + b_ref[...]
    x = res_ref[...] + out
    x_ref[...] = x
    h_ref[...] = _rms_rows(x, g_ref[...], RMS_EPS).astype(h_ref.dtype)


def _conv_prompt(u, w_dw, b_dw, ln_g, ln_b, w2, b2, res, g, seq, tm):
    M, ch = u.shape
    width = w_dw.shape[0]
    halo = -(-(width - 1) // SUBLANES) * SUBLANES
    row = lambda m: (m, 0)
    fixed = lambda m: (0, 0)
    vec = pl.BlockSpec((1, ch), fixed)
    return pl.pallas_call(
        functools.partial(_conv_prompt_kernel, tiles_per_seq=seq // tm, halo=halo, chunk=32),
        name="conv_prompt",
        grid=(M // tm,),
        in_specs=[pl.BlockSpec((tm, ch), row),
                  pl.BlockSpec((halo, ch), lambda m: (jnp.maximum(m * (tm // halo) - 1, 0), 0)),
                  pl.BlockSpec((width, ch), fixed), vec, vec, vec,
                  pl.BlockSpec(w2.shape, fixed), vec,
                  pl.BlockSpec((tm, ch), row), vec],
        out_specs=[pl.BlockSpec((tm, ch), row)] * 2,
        out_shape=[jax.ShapeDtypeStruct((M, ch), F32), jax.ShapeDtypeStruct((M, ch), BF16)],
        scratch_shapes=[pltpu.VMEM((SUBLANES, halo + tm, ch), F32), pltpu.VMEM((tm, ch), F32),
                        pltpu.VMEM((width, SUBLANES, ch), F32)],
        compiler_params=_params("parallel"),
    )(u, u, w_dw, b_dw.reshape(1, ch), ln_g.reshape(1, ch), ln_b.reshape(1, ch), w2,
      b2.reshape(1, ch), res, g.reshape(1, ch))


def _ffn_prompt_kernel(h_ref, wu_ref, wdw_ref, bdw_ref, wd_ref, res_ref, g_ref,
                       *refs, tiles_per_seq, emit_x, sub):
    if emit_x:
        x_ref, hn_ref, st_ref, carry, gbuf = refs
    else:
        hn_ref, st_ref, carry, gbuf = refs
    m = pl.program_id(0)
    tm = h_ref.shape[0]
    d_ff = wd_ref.shape[0]
    pad = SUBLANES
    h = h_ref[...]
    seq_start = (m % tiles_per_seq) == 0
    for c0 in range(0, d_ff, sub):
        conv = []
        for half in range(2):
            cols = slice(half * d_ff + c0, half * d_ff + c0 + sub)
            u = jnp.dot(h, wu_ref[:, cols], preferred_element_type=F32)
            prev = jnp.where(seq_start, 0.0, carry[:, cols])
            carry[:, cols] = u[tm - pad:, :]
            st_ref[0, half, :, c0:c0 + sub] = u[tm - 2:, :]
            ext = jnp.concatenate([prev, u], axis=0)
            conv.append(wdw_ref[2:3, cols] * u + wdw_ref[1:2, cols] * ext[pad - 1:pad - 1 + tm]
                        + wdw_ref[0:1, cols] * ext[pad - 2:pad - 2 + tm] + bdw_ref[:, cols])
        gbuf[:, c0:c0 + sub] = ((conv[0] * jax.nn.sigmoid(conv[0])) * conv[1]).astype(BF16)
    y = jnp.dot(gbuf[...], wd_ref[...], preferred_element_type=F32)
    x = res_ref[...] + y
    if emit_x:
        x_ref[...] = x
    hn_ref[...] = _rms_rows(x, g_ref[...], RMS_EPS).astype(hn_ref.dtype)


def _ffn_prompt(h, w_up, w_dw, b_dw, w_down, res, g, batch, seq, tm, sub, h_dtype, emit_x):
    M, D = h.shape
    d_ff = w_down.shape[0]
    tiles_per_seq = seq // tm
    row = lambda m: (m, 0)
    fixed = lambda m: (0, 0)
    once = dict(pipeline_mode=pl.Buffered(1))
    out_specs = [pl.BlockSpec((tm, D), row),
                 pl.BlockSpec((1, 2, 2, d_ff), lambda m: (m, 0, 0, 0))]
    out_shape = [jax.ShapeDtypeStruct((M, D), h_dtype),
                 jax.ShapeDtypeStruct((M // tm, 2, 2, d_ff), F32)]
    if emit_x:
        out_specs = [pl.BlockSpec((tm, D), row)] + out_specs
        out_shape = [jax.ShapeDtypeStruct((M, D), F32)] + out_shape
    out = pl.pallas_call(
        functools.partial(_ffn_prompt_kernel, tiles_per_seq=tiles_per_seq, emit_x=emit_x, sub=sub),
        name="ffn_prompt",
        grid=(M // tm,),
        in_specs=[pl.BlockSpec((tm, D), row),
                  pl.BlockSpec((D, 2 * d_ff), fixed, **once),
                  pl.BlockSpec((3, 2 * d_ff), fixed, **once),
                  pl.BlockSpec((1, 2 * d_ff), fixed, **once),
                  pl.BlockSpec((d_ff, D), fixed, **once),
                  pl.BlockSpec((tm, D), row),
                  pl.BlockSpec((1, D), fixed, **once)],
        out_specs=out_specs, out_shape=out_shape,
        scratch_shapes=[pltpu.VMEM((SUBLANES, 2 * d_ff), F32), pltpu.VMEM((tm, d_ff), BF16)],
        compiler_params=_params("arbitrary"),
    )(h, w_up, w_dw, b_dw.reshape(1, 2 * d_ff), w_down, res, g.reshape(1, D))
    if emit_x:
        x, hn, st = out
    else:
        (hn, st), x = out, None
    st = st[tiles_per_seq - 1::tiles_per_seq].transpose(0, 2, 1, 3).reshape(batch, 2, 2 * d_ff)
    return x, hn, st


def _mm_plain_kernel(a_ref, w_ref, o_ref):
    o_ref[...] = jnp.dot(a_ref[...], w_ref[...], preferred_element_type=F32)


def _mm_plain(a, w, tn):
    M, K = a.shape
    N = w.shape[1]
    return pl.pallas_call(
        _mm_plain_kernel,
        name="mm_plain",
        grid=(N // tn,),
        in_specs=[pl.BlockSpec((M, K), lambda n: (0, 0)), pl.BlockSpec((K, tn), lambda n: (0, n))],
        out_specs=pl.BlockSpec((M, tn), lambda n: (0, n)),
        out_shape=jax.ShapeDtypeStruct((M, N), F32),
        compiler_params=_params("parallel"),
    )(a, w)


def _ffn_gate_sample_kernel(u_ref, s0_ref, s1_ref, w_ref, b_ref, o_ref, *, d_ff):
    cv = (w_ref[2:3, :] * u_ref[...] + w_ref[1:2, :] * s1_ref[...]
          + w_ref[0:1, :] * s0_ref[...] + b_ref[...])
    a = cv[:, 0:d_ff]
    o_ref[...] = ((a * jax.nn.sigmoid(a)) * cv[:, d_ff:2 * d_ff]).astype(o_ref.dtype)


def _ffn_gate_sample(u, state, w_dw, b_dw):
    B, two_ff = u.shape
    d_ff = two_ff // 2
    full = lambda s: pl.BlockSpec(s, lambda i: (0, 0))
    return pl.pallas_call(
        functools.partial(_ffn_gate_sample_kernel, d_ff=d_ff),
        name="ffn_gate_sample",
        grid=(1,),
        in_specs=[full((B, two_ff))] * 3 + [full((3, two_ff)), full((1, two_ff))],
        out_specs=full((B, d_ff)),
        out_shape=jax.ShapeDtypeStruct((B, d_ff), BF16),
        compiler_params=_params("arbitrary"),
    )(u, state[:, 0], state[:, 1], w_dw, b_dw.reshape(1, two_ff))


def _conv_sample_kernel(u_ref, st_ref, wdw_ref, bdw_ref, lng_ref, lnb_ref, o_ref):
    width = wdw_ref.shape[0]
    c = jnp.sum(st_ref[...] * wdw_ref[0:width - 1, :][None], axis=1)
    c = c + u_ref[...] * wdw_ref[width - 1:width, :] + bdw_ref[...]
    mu = jnp.mean(c, axis=-1, keepdims=True)
    xc = c - mu
    y = xc * lax.rsqrt(jnp.mean(xc * xc, axis=-1, keepdims=True) + LN_EPS)
    y = y * lng_ref[...] + lnb_ref[...]
    o_ref[...] = (y * jax.nn.sigmoid(y)).astype(o_ref.dtype)


def _conv_sample(u, state, w_dw, b_dw, ln_g, ln_b):
    B, ch = u.shape
    full2 = lambda s: pl.BlockSpec(s, lambda i: (0, 0))
    vec = full2((1, ch))
    return pl.pallas_call(
        _conv_sample_kernel,
        name="conv_sample",
        grid=(1,),
        in_specs=[full2((B, ch)), pl.BlockSpec(state.shape, lambda i: (0, 0, 0)),
                  full2(w_dw.shape), vec, vec, vec],
        out_specs=full2((B, ch)),
        out_shape=jax.ShapeDtypeStruct((B, ch), BF16),
        compiler_params=_params("arbitrary"),
    )(u, state, w_dw, b_dw.reshape(1, ch), ln_g.reshape(1, ch), ln_b.reshape(1, ch))


def kernel(x_prompt, x_sample, cache_k, cache_v, page_table, state_conv, state_ffn, norm_mix, norm_ffn, norm_final, w_qkv, w_o, lambda_q1, lambda_k1, lambda_q2, lambda_k2, subln_g, w_pw1, b_pw1, w_dw, b_dw, ln_g, ln_b, w_pw2, b_pw2, w_up, w_ffn_dw, b_ffn_dw, w_down):
    Bp, T, D = x_prompt.shape
    Bs, Ts, _ = x_sample.shape
    assert Ts == 1, "sample group is one token per sequence"
    depth = norm_mix.shape[0]
    head_dim = lambda_q1.shape[1]
    vd = subln_g.shape[1]
    n_heads = w_o.shape[1] // vd
    scale = head_dim ** -0.5 * math.log2(math.e)
    Mp, Ms = Bp * T, Bs * Ts
    TM = 512

    xp = x_prompt.reshape(Mp, D)
    xs = x_sample.reshape(Ms, D)
    hp = hs = None

    kp_l, vp_l, ks_l, vs_l = [], [], [], []
    cp_l, cs_l, fp_l, fs_l = [], [], [], []
    for i in range(depth):
        j = i // N_MIXERS
        if i % N_MIXERS == 0:
            lam_init = 0.8 - 0.6 * math.exp(-0.3 * i)
            lams = (lambda_q1[j], lambda_k1[j], lambda_q2[j], lambda_k2[j])
            wq = _layer_bf16(w_qkv, j)
            wo = _layer_bf16(w_o, j)
            qp, kp, vp = _qkv(xp, norm_mix[i], wq, TM, scale, BF16)
            qs, ks, vs = _qkv(xs, norm_mix[i], wq, Ms, scale, F32)
            op = _attn_prompt(qp, kp, vp, lams, subln_g[j], Bp, T, n_heads, head_dim, lam_init, 512, 4)
            os_ = _decode_attn(qs, ks, vs, cache_k, cache_v, j, page_table, lams, subln_g[j],
                               n_heads, head_dim, lam_init, 8)
            xp, hp = _mm_res(op, wo, None, xp, norm_ffn[i], TM, BF16)
            xs, hs = _mm_res(os_.reshape(Ms, n_heads * vd), wo, None, xs, norm_ffn[i], Ms, BF16)
            kp_l.append(kp.reshape(Bp, T, n_heads, vd))
            vp_l.append(vp.reshape(Bp, T, n_heads, vd))
            ks_l.append(ks.reshape(Bs, Ts, n_heads, vd))
            vs_l.append(vs.reshape(Bs, Ts, n_heads, vd))
        else:
            w1 = _layer_bf16(w_pw1, j)
            w2 = _layer_bf16(w_pw2, j)
            width = w_dw.shape[1]
            up = _glu(hp, w1, b_pw1[j], TM)
            us = _glu(hs, w1, b_pw1[j], Ms)
            cs = _conv_sample(us, state_conv[j], w_dw[j], b_dw[j], ln_g[j], ln_b[j])
            xp, hp = _conv_prompt(up, w_dw[j], b_dw[j], ln_g[j], ln_b[j], w2, b_pw2[j], xp,
                                  norm_ffn[i], T, TM)
            xs, hs = _mm_res(cs, w2, b_pw2[j], xs, norm_ffn[i], Ms, BF16)
            cp_l.append(up.reshape(Bp, T, -1)[:, T - (width - 1):])
            cs_l.append(jnp.concatenate([state_conv[j][:, 1:], us[:, None]], axis=1))
        last = i == depth - 1
        g_next = norm_final if last else norm_mix[i + 1]
        h_dtype = F32 if last else BF16
        wu = _layer_bf16(w_up, i)
        wd = _layer_bf16(w_down, i)
        d_ff = wd.shape[0]
        xp, hp, fbp = _ffn_prompt(hp, wu, w_ffn_dw[i], b_ffn_dw[i], wd, xp, g_next, Bp, T,
                                  2 * TM, 256, h_dtype, not last)
        u_s = _mm_plain(hs, wu, d_ff // 2)
        gs = _ffn_gate_sample(u_s, state_ffn[i], w_ffn_dw[i], b_ffn_dw[i])
        xs, hs = _mm_res(gs, wd, None, xs, g_next, Ms, h_dtype, emit_x=not last)
        fp_l.append(fbp)
        fs_l.append(jnp.concatenate([state_ffn[i][:, 1:], u_s[:, None]], axis=1))

    y_prompt = hp.reshape(Bp, T, D)
    y_sample = hs.reshape(Bs, Ts, D)
    return (y_prompt, y_sample, jnp.stack(kp_l), jnp.stack(vp_l), jnp.stack(ks_l), jnp.stack(vs_l),
            jnp.stack(cp_l), jnp.stack(cs_l), jnp.stack(fp_l), jnp.stack(fs_l))
```

```python
import functools
import math

import jax
import jax.numpy as jnp
from jax import lax
from jax.experimental import pallas as pl
from jax.experimental.pallas import tpu as pltpu

F32 = jnp.float32
BF16 = jnp.bfloat16

RMS_EPS = 1e-6
LN_EPS = 1e-5
N_MIXERS = 2
LANES = 128
SUBLANES = 8
VMEM_LIMIT = 56 * 1024 * 1024
ATTN_REDUCE_WAYS = 4
WEIGHT_CAST_STEPS = 16

ROW_TILE = 512
WIDE_ROW_TILE = 1024
FFN_SLICE = 256
CONV_CHUNK = 32
ATTN_BLOCK = 512
ATTN_HEADS_PER_STEP = 4
DECODE_PAGES_PER_STEP = 16


def _params(*sem):
    return pltpu.CompilerParams(dimension_semantics=sem, vmem_limit_bytes=VMEM_LIMIT)


def _rms_rows(x, g, eps):
    ms = jnp.mean(x * x, axis=-1, keepdims=True)
    return x * lax.rsqrt(ms + eps) * g


def _diff_lambda(lq1, lk1, lq2, lk2, lam_init):
    a = jnp.sum(lq1[...] * lk1[...], keepdims=True)
    b = jnp.sum(lq2[...] * lk2[...], keepdims=True)
    return jnp.exp(a) - jnp.exp(b) + lam_init


def _cast_kernel(*refs):
    n = len(refs) // 2
    for w_ref, o_ref in zip(refs[:n], refs[n:]):
        o_ref[...] = w_ref[...].astype(o_ref.dtype)


def _layers_bf16(layers):
    in_specs, out_specs, out_shape = [], [], []
    for w, i in layers:
        _, R, C = w.shape
        tr = R // WEIGHT_CAST_STEPS
        assert tr % (2 * SUBLANES) == 0 and tr * WEIGHT_CAST_STEPS == R
        in_specs.append(pl.BlockSpec((None, tr, C), lambda r, i=i: (i, r, 0)))
        out_specs.append(pl.BlockSpec((tr, C), lambda r: (r, 0)))
        out_shape.append(jax.ShapeDtypeStruct((R, C), BF16))
    return pl.pallas_call(
        _cast_kernel,
        name="cast_bf16",
        grid=(WEIGHT_CAST_STEPS,),
        in_specs=in_specs, out_specs=out_specs, out_shape=out_shape,
        compiler_params=_params("parallel"),
    )(*[w for w, _ in layers])


def _qkv_kernel(x_ref, g_ref, w_ref, q_ref, k_ref, v_ref, *, width, scale):
    h = _rms_rows(x_ref[...], g_ref[...], RMS_EPS).astype(BF16)
    q = jnp.dot(h, w_ref[:, 0:width], preferred_element_type=F32)
    q_ref[...] = (q * scale).astype(q_ref.dtype)
    k_ref[...] = jnp.dot(h, w_ref[:, width:2 * width], preferred_element_type=F32)
    v_ref[...] = jnp.dot(h, w_ref[:, 2 * width:3 * width], preferred_element_type=F32)


def _qkv(x, g, w, tm, scale, q_dtype):
    M, D = x.shape
    width = w.shape[1] // 3
    blk = lambda m: (m, 0)
    return pl.pallas_call(
        functools.partial(_qkv_kernel, width=width, scale=scale),
        name="qkv",
        grid=(M // tm,),
        in_specs=[pl.BlockSpec((tm, D), blk),
                  pl.BlockSpec((1, D), lambda m: (0, 0)),
                  pl.BlockSpec(w.shape, lambda m: (0, 0))],
        out_specs=[pl.BlockSpec((tm, width), blk)] * 3,
        out_shape=[jax.ShapeDtypeStruct((M, width), q_dtype),
                   jax.ShapeDtypeStruct((M, width), F32),
                   jax.ShapeDtypeStruct((M, width), F32)],
        compiler_params=_params("parallel"),
    )(x, g.reshape(1, D), w)


def _attn_prompt_kernel(lq1, lk1, lq2, lk2, g_ref, q_ref, k_ref, v_ref, o_ref, kb, vt, acc, sbuf,
                        *, tq, head_dim, lam_init):
    T = q_ref.shape[0]
    vd = 2 * head_dim
    n_heads = q_ref.shape[1] // vd
    nq = T // tq
    streams = [(hh, c) for hh in range(n_heads) for c in range(2)]
    lam = _diff_lambda(lq1, lk1, lq2, lk2, lam_init)
    kb[...] = k_ref[...].astype(BF16)
    for hh in range(n_heads):
        for kj in range(nq):
            vt[hh, kj] = v_ref[kj * tq:(kj + 1) * tq, hh * vd:(hh + 1) * vd].T.astype(BF16)
    first = lax.broadcasted_iota(jnp.int32, (tq, vd), 1) < head_dim
    key_i = lax.broadcasted_iota(jnp.int32, (tq, tq), 0)
    qry_i = lax.broadcasted_iota(jnp.int32, (tq, tq), 1)
    causal = key_i <= qry_i
    g = g_ref[...]

    def reduce_keys(op, x):
        n = x.shape[0] // (SUBLANES * ATTN_REDUCE_WAYS)
        x = op(x.reshape(ATTN_REDUCE_WAYS, n, SUBLANES, x.shape[1]), axis=1)
        return op(op(x, axis=0), axis=0, keepdims=True)

    def scores(i, kj, q_parts):
        start = kj * tq if isinstance(kj, int) else pl.multiple_of(kj * tq, tq)
        hh = streams[i][0]
        kt = kb[pl.ds(start, tq), hh * vd:(hh + 1) * vd]
        return lax.dot_general(kt, q_parts[i], (((1,), (1,)), ((), ())),
                               preferred_element_type=F32)

    def consume(i, kj, state, s, masked):
        m, l = state
        if masked:
            s = jnp.where(causal, s, -jnp.inf)
        m_new = jnp.maximum(m, reduce_keys(jnp.max, s))
        p = jnp.exp2(s - m_new)
        alpha = jnp.exp2(m - m_new)
        l = alpha * l + reduce_keys(jnp.sum, p)
        acc[i] = alpha * acc[i] + jnp.dot(vt[streams[i][0], kj], p.astype(BF16),
                                          preferred_element_type=F32)
        return m_new, l

    def query_parts(qi):
        qstart = qi * tq if isinstance(qi, int) else pl.multiple_of(qi * tq, tq)
        parts = []
        for hh, c in streams:
            q = q_ref[pl.ds(qstart, tq), hh * vd:(hh + 1) * vd]
            zero = jnp.zeros_like(q)
            parts.append(jnp.where(first, zero, q) if c else jnp.where(first, q, zero))
        return parts

    def fill_scores(kj, q_parts):
        for i in range(len(streams)):
            sbuf[i] = scores(i, kj, q_parts)

    def q_block(qi, _):
        qstart = pl.multiple_of(qi * tq, tq)
        q_parts = query_parts(qi)
        acc[...] = jnp.zeros(acc.shape, F32)

        def kv_step(kj, carry):
            cur = [sbuf[i] for i in range(len(streams))]
            fill_scores(kj + 1, q_parts)
            return tuple(consume(i, kj, carry[i], cur[i], False) for i in range(len(streams)))

        init = tuple((jnp.full((1, tq), -jnp.inf, F32), jnp.zeros((1, tq), F32)) for _ in streams)
        carry = lax.fori_loop(0, qi, kv_step, init)
        diag = [sbuf[i] for i in range(len(streams))]
        fill_scores(0, query_parts(jnp.minimum(qi + 1, nq - 1)))
        fin = [consume(i, qi, carry[i], diag[i], True) for i in range(len(streams))]
        for hh in range(n_heads):
            l0, l1 = fin[2 * hh][1], fin[2 * hh + 1][1]
            o = (acc[2 * hh] / l0 - lam * (acc[2 * hh + 1] / l1)).T
            o = _rms_rows(o, g, RMS_EPS) * (1.0 - lam_init)
            o_ref[pl.ds(qstart, tq), hh * vd:(hh + 1) * vd] = o.astype(o_ref.dtype)
        return 0

    fill_scores(0, query_parts(0))
    lax.fori_loop(0, nq, q_block, 0)


def _attn_prompt(q, k, v, lams, g, batch, seq, n_heads, head_dim, lam_init, tq, heads_per_step):
    M, width = q.shape
    vd = width // n_heads
    hs = heads_per_step
    blk = pl.BlockSpec((seq, hs * vd), lambda b, h: (b, h))
    small = pl.BlockSpec((1, head_dim), lambda b, h: (0, 0))
    return pl.pallas_call(
        functools.partial(_attn_prompt_kernel, tq=tq, head_dim=head_dim, lam_init=lam_init),
        name="attn_prompt",
        grid=(batch, n_heads // hs),
        in_specs=[small, small, small, small,
                  pl.BlockSpec((1, vd), lambda b, h: (0, 0)), blk, blk, blk],
        out_specs=blk,
        out_shape=jax.ShapeDtypeStruct((M, width), BF16),
        scratch_shapes=[pltpu.VMEM((seq, hs * vd), BF16),
                        pltpu.VMEM((hs, seq // tq, vd, tq), BF16),
                        pltpu.VMEM((2 * hs, vd, tq), F32),
                        pltpu.VMEM((2 * hs, tq, tq), F32)],
        compiler_params=_params("parallel", "parallel"),
    )(*[x.reshape(1, head_dim) for x in lams], g.reshape(1, vd), q, k, v)


def _decode_attn_kernel(pt_ref, lq1, lk1, lq2, lk2, g_ref, q_ref, kn_ref, vn_ref, *refs,
                        n_pages, n_heads, head_dim, lam_init):
    del pt_ref
    k_refs = refs[:n_pages]
    v_refs = refs[n_pages:2 * n_pages]
    o_ref = refs[2 * n_pages]
    m_scr, l_scr, acc_scr = refs[2 * n_pages + 1:]
    c = pl.program_id(1)
    nc = pl.num_programs(1)
    rows_per_page, vd = k_refs[0].shape
    n_rows = 2 * n_heads

    @pl.when(c == 0)
    def _():
        m_scr[...] = jnp.full(m_scr.shape, -jnp.inf, F32)
        l_scr[...] = jnp.zeros(l_scr.shape, F32)
        acc_scr[...] = jnp.zeros(acc_scr.shape, F32)

    q = q_ref[...]
    lane = lax.broadcasted_iota(jnp.int32, q.shape, 1)
    q_rows = jnp.concatenate([jnp.where(lane < head_dim, q, 0.0),
                              jnp.where(lane < head_dim, 0.0, q)], axis=0).astype(BF16)
    assert n_heads & (n_heads - 1) == 0
    row_head = lax.broadcasted_iota(jnp.int32, (n_rows, rows_per_page), 0) & (n_heads - 1)
    col_head = lax.broadcasted_iota(jnp.int32, (n_rows, rows_per_page), 1) & (n_heads - 1)
    same_head = row_head == col_head

    scores = []
    for gi in range(n_pages):
        s = lax.dot_general(q_rows, k_refs[gi][...].astype(BF16), (((1,), (1,)), ((), ())),
                            preferred_element_type=F32)
        scores.append(jnp.where(same_head, s, -jnp.inf))
    m_old = m_scr[...]
    m_new = m_old
    for s in scores:
        m_new = jnp.maximum(m_new, jnp.max(s, axis=-1, keepdims=True))
    alpha = jnp.exp2(m_old - m_new)
    l_new = alpha * l_scr[...]
    acc = alpha * acc_scr[...]
    for gi in range(n_pages):
        p = jnp.exp2(scores[gi] - m_new)
        l_new = l_new + jnp.sum(p, axis=-1, keepdims=True)
        acc = acc + jnp.dot(p.astype(BF16), v_refs[gi][...].astype(BF16),
                            preferred_element_type=F32)
    m_scr[...] = m_new
    l_scr[...] = l_new
    acc_scr[...] = acc

    @pl.when(c == nc - 1)
    def _():
        lam = _diff_lambda(lq1, lk1, lq2, lk2, lam_init)
        kn2 = jnp.concatenate([kn_ref[...]] * 2, axis=0)
        vn2 = jnp.concatenate([vn_ref[...]] * 2, axis=0)
        s_new = jnp.sum(kn2 * q_rows.astype(F32), axis=-1, keepdims=True)
        m_fin = jnp.maximum(m_new, s_new)
        a_fin = jnp.exp2(m_new - m_fin)
        p_new = jnp.exp2(s_new - m_fin)
        l_fin = a_fin * l_new + p_new
        o2 = (a_fin * acc + p_new * vn2) / l_fin
        o = o2[0:n_heads] - lam * o2[n_heads:n_rows]
        o_ref[...] = _rms_rows(o, g_ref[...], RMS_EPS) * (1.0 - lam_init)


def _decode_attn(q, k_new, v_new, cache_k, cache_v, j, page_table, lams, g,
                 n_heads, head_dim, lam_init, pages_per_step):
    B = q.shape[0]
    vd = 2 * head_dim
    n_pages = page_table.shape[1]
    n_layers, n_pool, page = cache_k.shape[:3]
    G = pages_per_step
    small = pl.BlockSpec((1, head_dim), lambda b, c, pt: (0, 0))
    per_b = pl.BlockSpec((None, n_heads, vd), lambda b, c, pt: (b, 0, 0))
    ck = cache_k.reshape(n_layers, n_pool, page * n_heads, vd)
    cv = cache_v.reshape(n_layers, n_pool, page * n_heads, vd)

    def page_spec(gi):
        return pl.BlockSpec((None, None, page * n_heads, vd),
                            lambda b, c, pt: (j, pt[b, c * G + gi], 0, 0))

    grid_spec = pltpu.PrefetchScalarGridSpec(
        num_scalar_prefetch=1,
        grid=(B, n_pages // G),
        in_specs=[small, small, small, small,
                  pl.BlockSpec((1, vd), lambda b, c, pt: (0, 0)),
                  per_b, per_b, per_b]
                 + [page_spec(gi) for gi in range(G)] * 2,
        out_specs=per_b,
        scratch_shapes=[pltpu.VMEM((2 * n_heads, 1), F32), pltpu.VMEM((2 * n_heads, 1), F32),
                        pltpu.VMEM((2 * n_heads, vd), F32)],
    )
    return pl.pallas_call(
        functools.partial(_decode_attn_kernel, n_pages=G, n_heads=n_heads, head_dim=head_dim,
                          lam_init=lam_init),
        name="attn_decode",
        grid_spec=grid_spec,
        out_shape=jax.ShapeDtypeStruct((B, n_heads, vd), F32),
        compiler_params=_params("parallel", "arbitrary"),
    )(page_table, *[x.reshape(1, head_dim) for x in lams], g.reshape(1, vd),
      q.reshape(B, n_heads, vd), k_new.reshape(B, n_heads, vd), v_new.reshape(B, n_heads, vd),
      *([ck] * G), *([cv] * G))


def _mm_res_kernel(*refs, has_bias, emit_x):
    a_ref, w_ref = refs[:2]
    i = 2
    b_ref = None
    if has_bias:
        b_ref = refs[i]
        i += 1
    res_ref, g_ref = refs[i:i + 2]
    outs = refs[i + 2:]
    y = jnp.dot(a_ref[...].astype(BF16), w_ref[...], preferred_element_type=F32)
    if has_bias:
        y = y + b_ref[...]
    x = res_ref[...] + y
    if emit_x:
        outs[0][...] = x
    outs[-1][...] = _rms_rows(x, g_ref[...], RMS_EPS).astype(outs[-1].dtype)


def _mm_res(a, w, bias, res, g, tm, h_dtype, emit_x=True):
    M, K = a.shape
    N = w.shape[1]
    row = lambda m: (m, 0)
    fixed = lambda m: (0, 0)
    ins = [a, w]
    in_specs = [pl.BlockSpec((tm, K), row), pl.BlockSpec((K, N), fixed)]
    if bias is not None:
        ins.append(bias.reshape(1, N))
        in_specs.append(pl.BlockSpec((1, N), fixed))
    ins += [res, g.reshape(1, N)]
    in_specs += [pl.BlockSpec((tm, N), row), pl.BlockSpec((1, N), fixed)]
    out_specs = [pl.BlockSpec((tm, N), row)]
    out_shape = [jax.ShapeDtypeStruct((M, N), h_dtype)]
    if emit_x:
        out_specs = [pl.BlockSpec((tm, N), row)] + out_specs
        out_shape = [jax.ShapeDtypeStruct((M, N), F32)] + out_shape
    out = pl.pallas_call(
        functools.partial(_mm_res_kernel, has_bias=bias is not None, emit_x=emit_x),
        name="mm_res",
        grid=(M // tm,),
        in_specs=in_specs, out_specs=out_specs, out_shape=out_shape,
        compiler_params=_params("parallel"),
    )(*ins)
    return out if emit_x else (None, out[0])


def _glu_kernel(a_ref, w_ref, b_ref, u_ref, *, ch):
    a = a_ref[...]
    lin = jnp.dot(a, w_ref[:, 0:ch], preferred_element_type=F32) + b_ref[:, 0:ch]
    gate = jnp.dot(a, w_ref[:, ch:2 * ch], preferred_element_type=F32) + b_ref[:, ch:2 * ch]
    u_ref[...] = lin * jax.nn.sigmoid(gate)


def _glu(a, w, b, tm):
    M, K = a.shape
    ch = w.shape[1] // 2
    return pl.pallas_call(
        functools.partial(_glu_kernel, ch=ch),
        name="glu",
        grid=(M // tm,),
        in_specs=[pl.BlockSpec((tm, K), lambda m: (m, 0)),
                  pl.BlockSpec(w.shape, lambda m: (0, 0)),
                  pl.BlockSpec((1, 2 * ch), lambda m: (0, 0))],
        out_specs=pl.BlockSpec((tm, ch), lambda m: (m, 0)),
        out_shape=jax.ShapeDtypeStruct((M, ch), F32),
        compiler_params=_params("parallel"),
    )(a, w, b.reshape(1, 2 * ch))


def _conv_prompt_kernel(u_ref, halo_ref, wdw_ref, bdw_ref, lng_ref, lnb_ref, w_ref, b_ref,
                        res_ref, g_ref, x_ref, h_ref, ubuf, cbuf, wb, *, tiles_per_seq, halo, chunk):
    m = pl.program_id(0)
    tm, ch = u_ref.shape
    width = wdw_ref.shape[0]
    seq_start = (m % tiles_per_seq) == 0
    ubuf[0, 0:halo, :] = jnp.where(seq_start, 0.0, halo_ref[...])
    ubuf[0, halo:halo + tm, :] = u_ref[...]
    n_rows = halo + tm
    base = ubuf[0]
    for r in range(1, SUBLANES):
        ubuf[r] = pltpu.roll(base, n_rows - r, axis=0)
    off = halo - (width - 1)
    wb[...] = jnp.broadcast_to(wdw_ref[...][:, None, :], wb.shape)
    bdw = bdw_ref[...]
    groups = chunk // SUBLANES
    for c0 in range(0, tm, chunk):
        acc = jnp.broadcast_to(bdw, (groups, SUBLANES, ch))
        for j in range(width):
            a, r = divmod(off + j, SUBLANES)
            win = ubuf[r, c0 + a * SUBLANES:c0 + a * SUBLANES + chunk, :]
            acc = acc + wb[j][None] * win.reshape(groups, SUBLANES, ch)
        cbuf[c0:c0 + chunk, :] = acc.reshape(chunk, ch)
    c = cbuf[...]
    mu = jnp.mean(c, axis=-1, keepdims=True)
    xc = c - mu
    y = xc * lax.rsqrt(jnp.mean(xc * xc, axis=-1, keepdims=True) + LN_EPS)
    y = y * lng_ref[...] + lnb_ref[...]
    y = y * jax.nn.sigmoid(y)
    out = jnp.dot(y.astype(BF16), w_ref[...], preferred_element_type=F32) + b_ref[...]
    x = res_ref[...] + out
    x_ref[...] = x
    h_ref[...] = _rms_rows(x, g_ref[...], RMS_EPS).astype(h_ref.dtype)


def _conv_prompt(u, w_dw, b_dw, ln_g, ln_b, w2, b2, res, g, seq, tm):
    M, ch = u.shape
    width = w_dw.shape[0]
    halo = -(-(width - 1) // SUBLANES) * SUBLANES
    row = lambda m: (m, 0)
    fixed = lambda m: (0, 0)
    vec = pl.BlockSpec((1, ch), fixed)
    return pl.pallas_call(
        functools.partial(_conv_prompt_kernel, tiles_per_seq=seq // tm, halo=halo,
                          chunk=CONV_CHUNK),
        name="conv_prompt",
        grid=(M // tm,),
        in_specs=[pl.BlockSpec((tm, ch), row),
                  pl.BlockSpec((halo, ch), lambda m: (jnp.maximum(m * (tm // halo) - 1, 0), 0)),
                  pl.BlockSpec((width, ch), fixed), vec, vec, vec,
                  pl.BlockSpec(w2.shape, fixed), vec,
                  pl.BlockSpec((tm, ch), row), vec],
        out_specs=[pl.BlockSpec((tm, ch), row)] * 2,
        out_shape=[jax.ShapeDtypeStruct((M, ch), F32), jax.ShapeDtypeStruct((M, ch), BF16)],
        scratch_shapes=[pltpu.VMEM((SUBLANES, halo + tm, ch), F32), pltpu.VMEM((tm, ch), F32),
                        pltpu.VMEM((width, SUBLANES, ch), F32)],
        compiler_params=_params("parallel"),
    )(u, u, w_dw, b_dw.reshape(1, ch), ln_g.reshape(1, ch), ln_b.reshape(1, ch), w2,
      b2.reshape(1, ch), res, g.reshape(1, ch))


def _ffn_prompt_kernel(h_ref, wu_ref, wdw_ref, bdw_ref, wd_ref, res_ref, g_ref,
                       *refs, tiles_per_seq, emit_x, sub):
    if emit_x:
        x_ref, hn_ref, st_ref, carry, gbuf = refs
    else:
        hn_ref, st_ref, carry, gbuf = refs
    m = pl.program_id(0)
    tm = h_ref.shape[0]
    d_ff = wd_ref.shape[0]
    pad = SUBLANES
    h = h_ref[...]
    seq_start = (m % tiles_per_seq) == 0
    for c0 in range(0, d_ff, sub):
        conv = []
        for half in range(2):
            cols = slice(half * d_ff + c0, half * d_ff + c0 + sub)
            u = jnp.dot(h, wu_ref[:, cols], preferred_element_type=F32)
            prev = jnp.where(seq_start, 0.0, carry[:, cols])
            carry[:, cols] = u[tm - pad:, :]
            st_ref[0, half, :, c0:c0 + sub] = u[tm - 2:, :]
            ext = jnp.concatenate([prev, u], axis=0)
            conv.append(wdw_ref[2:3, cols] * u + wdw_ref[1:2, cols] * ext[pad - 1:pad - 1 + tm]
                        + wdw_ref[0:1, cols] * ext[pad - 2:pad - 2 + tm] + bdw_ref[:, cols])
        gbuf[:, c0:c0 + sub] = ((conv[0] * jax.nn.sigmoid(conv[0])) * conv[1]).astype(BF16)
    y = jnp.dot(gbuf[...], wd_ref[...], preferred_element_type=F32)
    x = res_ref[...] + y
    if emit_x:
        x_ref[...] = x
    hn_ref[...] = _rms_rows(x, g_ref[...], RMS_EPS).astype(hn_ref.dtype)


def _ffn_prompt(h, w_up, w_dw, b_dw, w_down, res, g, batch, seq, tm, sub, h_dtype, emit_x):
    M, D = h.shape
    d_ff = w_down.shape[0]
    tiles_per_seq = seq // tm
    row = lambda m: (m, 0)
    fixed = lambda m: (0, 0)
    once = dict(pipeline_mode=pl.Buffered(1))
    out_specs = [pl.BlockSpec((tm, D), row),
                 pl.BlockSpec((1, 2, 2, d_ff), lambda m: (m, 0, 0, 0))]
    out_shape = [jax.ShapeDtypeStruct((M, D), h_dtype),
                 jax.ShapeDtypeStruct((M // tm, 2, 2, d_ff), F32)]
    if emit_x:
        out_specs = [pl.BlockSpec((tm, D), row)] + out_specs
        out_shape = [jax.ShapeDtypeStruct((M, D), F32)] + out_shape
    out = pl.pallas_call(
        functools.partial(_ffn_prompt_kernel, tiles_per_seq=tiles_per_seq, emit_x=emit_x, sub=sub),
        name="ffn_prompt",
        grid=(M // tm,),
        in_specs=[pl.BlockSpec((tm, D), row),
                  pl.BlockSpec((D, 2 * d_ff), fixed, **once),
                  pl.BlockSpec((3, 2 * d_ff), fixed, **once),
                  pl.BlockSpec((1, 2 * d_ff), fixed, **once),
                  pl.BlockSpec((d_ff, D), fixed, **once),
                  pl.BlockSpec((tm, D), row),
                  pl.BlockSpec((1, D), fixed, **once)],
        out_specs=out_specs, out_shape=out_shape,
        scratch_shapes=[pltpu.VMEM((SUBLANES, 2 * d_ff), F32), pltpu.VMEM((tm, d_ff), BF16)],
        compiler_params=_params("arbitrary"),
    )(h, w_up, w_dw, b_dw.reshape(1, 2 * d_ff), w_down, res, g.reshape(1, D))
    if emit_x:
        x, hn, st = out
    else:
        (hn, st), x = out, None
    st = st[tiles_per_seq - 1::tiles_per_seq].transpose(0, 2, 1, 3).reshape(batch, 2, 2 * d_ff)
    return x, hn, st


def _mm_plain_kernel(a_ref, w_ref, o_ref):
    o_ref[...] = jnp.dot(a_ref[...], w_ref[...], preferred_element_type=F32)


def _mm_plain(a, w, tn):
    M, K = a.shape
    N = w.shape[1]
    return pl.pallas_call(
        _mm_plain_kernel,
        name="mm_plain",
        grid=(N // tn,),
        in_specs=[pl.BlockSpec((M, K), lambda n: (0, 0)), pl.BlockSpec((K, tn), lambda n: (0, n))],
        out_specs=pl.BlockSpec((M, tn), lambda n: (0, n)),
        out_shape=jax.ShapeDtypeStruct((M, N), F32),
        compiler_params=_params("parallel"),
    )(a, w)


def _ffn_gate_sample_kernel(u_ref, s0_ref, s1_ref, w_ref, b_ref, o_ref, *, d_ff):
    cv = (w_ref[2:3, :] * u_ref[...] + w_ref[1:2, :] * s1_ref[...]
          + w_ref[0:1, :] * s0_ref[...] + b_ref[...])
    a = cv[:, 0:d_ff]
    o_ref[...] = ((a * jax.nn.sigmoid(a)) * cv[:, d_ff:2 * d_ff]).astype(o_ref.dtype)


def _ffn_gate_sample(u, state, w_dw, b_dw):
    B, two_ff = u.shape
    d_ff = two_ff // 2
    full = lambda s: pl.BlockSpec(s, lambda i: (0, 0))
    return pl.pallas_call(
        functools.partial(_ffn_gate_sample_kernel, d_ff=d_ff),
        name="ffn_gate_sample",
        grid=(1,),
        in_specs=[full((B, two_ff))] * 3 + [full((3, two_ff)), full((1, two_ff))],
        out_specs=full((B, d_ff)),
        out_shape=jax.ShapeDtypeStruct((B, d_ff), BF16),
        compiler_params=_params("arbitrary"),
    )(u, state[:, 0], state[:, 1], w_dw, b_dw.reshape(1, two_ff))


def _conv_sample_kernel(u_ref, st_ref, wdw_ref, bdw_ref, lng_ref, lnb_ref, o_ref):
    width = wdw_ref.shape[0]
    c = jnp.sum(st_ref[...] * wdw_ref[0:width - 1, :][None], axis=1)
    c = c + u_ref[...] * wdw_ref[width - 1:width, :] + bdw_ref[...]
    mu = jnp.mean(c, axis=-1, keepdims=True)
    xc = c - mu
    y = xc * lax.rsqrt(jnp.mean(xc * xc, axis=-1, keepdims=True) + LN_EPS)
    y = y * lng_ref[...] + lnb_ref[...]
    o_ref[...] = (y * jax.nn.sigmoid(y)).astype(o_ref.dtype)


def _conv_sample(u, state, w_dw, b_dw, ln_g, ln_b):
    B, ch = u.shape
    full2 = lambda s: pl.BlockSpec(s, lambda i: (0, 0))
    vec = full2((1, ch))
    return pl.pallas_call(
        _conv_sample_kernel,
        name="conv_sample",
        grid=(1,),
        in_specs=[full2((B, ch)), pl.BlockSpec(state.shape, lambda i: (0, 0, 0)),
                  full2(w_dw.shape), vec, vec, vec],
        out_specs=full2((B, ch)),
        out_shape=jax.ShapeDtypeStruct((B, ch), BF16),
        compiler_params=_params("arbitrary"),
    )(u, state, w_dw, b_dw.reshape(1, ch), ln_g.reshape(1, ch), ln_b.reshape(1, ch))


def kernel(x_prompt, x_sample, cache_k, cache_v, page_table, state_conv, state_ffn, norm_mix, norm_ffn, norm_final, w_qkv, w_o, lambda_q1, lambda_k1, lambda_q2, lambda_k2, subln_g, w_pw1, b_pw1, w_dw, b_dw, ln_g, ln_b, w_pw2, b_pw2, w_up, w_ffn_dw, b_ffn_dw, w_down):
    Bp, T, D = x_prompt.shape
    Bs, Ts, _ = x_sample.shape
    assert Ts == 1, "sample group is one token per sequence"
    depth = norm_mix.shape[0]
    head_dim = lambda_q1.shape[1]
    vd = subln_g.shape[1]
    n_heads = w_o.shape[1] // vd
    scale = head_dim ** -0.5 * math.log2(math.e)
    Mp, Ms = Bp * T, Bs * Ts

    xp = x_prompt.reshape(Mp, D)
    xs = x_sample.reshape(Ms, D)
    hp = hs = None

    wanted = []
    for i in range(depth):
        j = i // N_MIXERS
        wanted += [(w_qkv, j), (w_o, j)] if i % N_MIXERS == 0 else [(w_pw1, j), (w_pw2, j)]
        wanted += [(w_up, i), (w_down, i)]
    bf16_of = dict(zip([(id(w), i) for w, i in wanted], _layers_bf16(wanted)))

    kp_l, vp_l, ks_l, vs_l = [], [], [], []
    cp_l, cs_l, fp_l, fs_l = [], [], [], []
    for i in range(depth):
        j = i // N_MIXERS
        if i % N_MIXERS == 0:
            lam_init = 0.8 - 0.6 * math.exp(-0.3 * i)
            lams = (lambda_q1[j], lambda_k1[j], lambda_q2[j], lambda_k2[j])
            wq = bf16_of[id(w_qkv), j]
            wo = bf16_of[id(w_o), j]
            qp, kp, vp = _qkv(xp, norm_mix[i], wq, ROW_TILE, scale, BF16)
            qs, ks, vs = _qkv(xs, norm_mix[i], wq, Ms, scale, F32)
            op = _attn_prompt(qp, kp, vp, lams, subln_g[j], Bp, T, n_heads, head_dim, lam_init,
                              ATTN_BLOCK, ATTN_HEADS_PER_STEP)
            os_ = _decode_attn(qs, ks, vs, cache_k, cache_v, j, page_table, lams, subln_g[j],
                               n_heads, head_dim, lam_init, DECODE_PAGES_PER_STEP)
            xp, hp = _mm_res(op, wo, None, xp, norm_ffn[i], WIDE_ROW_TILE, BF16)
            xs, hs = _mm_res(os_.reshape(Ms, n_heads * vd), wo, None, xs, norm_ffn[i], Ms, BF16)
            kp_l.append(kp.reshape(Bp, T, n_heads, vd))
            vp_l.append(vp.reshape(Bp, T, n_heads, vd))
            ks_l.append(ks.reshape(Bs, Ts, n_heads, vd))
            vs_l.append(vs.reshape(Bs, Ts, n_heads, vd))
        else:
            w1 = bf16_of[id(w_pw1), j]
            w2 = bf16_of[id(w_pw2), j]
            width = w_dw.shape[1]
            up = _glu(hp, w1, b_pw1[j], ROW_TILE)
            us = _glu(hs, w1, b_pw1[j], Ms)
            cs = _conv_sample(us, state_conv[j], w_dw[j], b_dw[j], ln_g[j], ln_b[j])
            xp, hp = _conv_prompt(up, w_dw[j], b_dw[j], ln_g[j], ln_b[j], w2, b_pw2[j], xp,
                                  norm_ffn[i], T, ROW_TILE)
            xs, hs = _mm_res(cs, w2, b_pw2[j], xs, norm_ffn[i], Ms, BF16)
            cp_l.append(up.reshape(Bp, T, -1)[:, T - (width - 1):])
            cs_l.append(jnp.concatenate([state_conv[j][:, 1:], us[:, None]], axis=1))
        last = i == depth - 1
        g_next = norm_final if last else norm_mix[i + 1]
        h_dtype = F32 if last else BF16
        wu = bf16_of[id(w_up), i]
        wd = bf16_of[id(w_down), i]
        d_ff = wd.shape[0]
        xp, hp, fbp = _ffn_prompt(hp, wu, w_ffn_dw[i], b_ffn_dw[i], wd, xp, g_next, Bp, T,
                                  WIDE_ROW_TILE, FFN_SLICE, h_dtype, not last)
        u_s = _mm_plain(hs, wu, d_ff // 2)
        gs = _ffn_gate_sample(u_s, state_ffn[i], w_ffn_dw[i], b_ffn_dw[i])
        xs, hs = _mm_res(gs, wd, None, xs, g_next, Ms, h_dtype, emit_x=not last)
        fp_l.append(fbp)
        fs_l.append(jnp.concatenate([state_ffn[i][:, 1:], u_s[:, None]], axis=1))

    y_prompt = hp.reshape(Bp, T, D)
    y_sample = hs.reshape(Bs, Ts, D)
    return (y_prompt, y_sample, jnp.stack(kp_l), jnp.stack(vp_l), jnp.stack(ks_l), jnp.stack(vs_l),
            jnp.stack(cp_l), jnp.stack(cs_l), jnp.stack(fp_l), jnp.stack(fs_l))
```

```python
import functools
import math

import jax
import jax.numpy as jnp
from jax import lax
from jax.experimental import pallas as pl
from jax.experimental.pallas import tpu as pltpu

F32 = jnp.float32
BF16 = jnp.bfloat16

RMS_EPS = 1e-6
LN_EPS = 1e-5
N_MIXERS = 2
LANES = 128
SUBLANES = 8
VMEM_LIMIT = 56 * 1024 * 1024
ATTN_REDUCE_WAYS = 4
WEIGHT_CAST_STEPS = 16

ROW_TILE = 512
WIDE_ROW_TILE = 1024
FFN_SLICE = 256
CONV_CHUNK = 32
ATTN_BLOCK = 512
ATTN_HEADS_PER_STEP = 4
DECODE_PAGES_PER_STEP = 16


def _params(*sem):
    return pltpu.CompilerParams(dimension_semantics=sem, vmem_limit_bytes=VMEM_LIMIT)


def _rms_rows(x, g, eps):
    ms = jnp.mean(x * x, axis=-1, keepdims=True)
    return x * lax.rsqrt(ms + eps) * g


def _diff_lambda(lq1, lk1, lq2, lk2, lam_init):
    a = jnp.sum(lq1[...] * lk1[...], keepdims=True)
    b = jnp.sum(lq2[...] * lk2[...], keepdims=True)
    return jnp.exp(a) - jnp.exp(b) + lam_init


def _cast_kernel(*refs):
    n = len(refs) // 2
    for w_ref, o_ref in zip(refs[:n], refs[n:]):
        o_ref[...] = w_ref[...].astype(o_ref.dtype)


def _layers_bf16(layers):
    in_specs, out_specs, out_shape = [], [], []
    for w, i in layers:
        _, R, C = w.shape
        tr = R // WEIGHT_CAST_STEPS
        assert tr % (2 * SUBLANES) == 0 and tr * WEIGHT_CAST_STEPS == R
        in_specs.append(pl.BlockSpec((None, tr, C), lambda r, i=i: (i, r, 0)))
        out_specs.append(pl.BlockSpec((tr, C), lambda r: (r, 0)))
        out_shape.append(jax.ShapeDtypeStruct((R, C), BF16))
    return pl.pallas_call(
        _cast_kernel,
        name="cast_bf16",
        grid=(WEIGHT_CAST_STEPS,),
        in_specs=in_specs, out_specs=out_specs, out_shape=out_shape,
        compiler_params=_params("parallel"),
    )(*[w for w, _ in layers])


def _qkv_kernel(x_ref, g_ref, w_ref, q_ref, k_ref, v_ref, *, width, scale):
    h = _rms_rows(x_ref[...], g_ref[...], RMS_EPS).astype(BF16)
    q = jnp.dot(h, w_ref[:, 0:width], preferred_element_type=F32)
    q_ref[...] = (q * scale).astype(q_ref.dtype)
    k_ref[...] = jnp.dot(h, w_ref[:, width:2 * width], preferred_element_type=F32)
    v_ref[...] = jnp.dot(h, w_ref[:, 2 * width:3 * width], preferred_element_type=F32)


def _qkv(x, g, w, tm, scale, q_dtype):
    M, D = x.shape
    width = w.shape[1] // 3
    blk = lambda m: (m, 0)
    return pl.pallas_call(
        functools.partial(_qkv_kernel, width=width, scale=scale),
        name="qkv",
        grid=(M // tm,),
        in_specs=[pl.BlockSpec((tm, D), blk),
                  pl.BlockSpec((1, D), lambda m: (0, 0)),
                  pl.BlockSpec(w.shape, lambda m: (0, 0))],
        out_specs=[pl.BlockSpec((tm, width), blk)] * 3,
        out_shape=[jax.ShapeDtypeStruct((M, width), q_dtype),
                   jax.ShapeDtypeStruct((M, width), F32),
                   jax.ShapeDtypeStruct((M, width), F32)],
        compiler_params=_params("parallel"),
    )(x, g.reshape(1, D), w)


def _attn_prompt_kernel(lq1, lk1, lq2, lk2, g_ref, q_ref, k_ref, v_ref, o_ref, kb, vt, acc, sbuf,
                        *, tq, head_dim, lam_init):
    T = q_ref.shape[0]
    vd = 2 * head_dim
    n_heads = q_ref.shape[1] // vd
    nq = T // tq
    streams = [(hh, c) for hh in range(n_heads) for c in range(2)]
    lam = _diff_lambda(lq1, lk1, lq2, lk2, lam_init)
    kb[...] = k_ref[...].astype(BF16)
    for hh in range(n_heads):
        for kj in range(nq):
            vt[hh, kj] = v_ref[kj * tq:(kj + 1) * tq, hh * vd:(hh + 1) * vd].T.astype(BF16)
    first = lax.broadcasted_iota(jnp.int32, (tq, vd), 1) < head_dim
    key_i = lax.broadcasted_iota(jnp.int32, (tq, tq), 0)
    qry_i = lax.broadcasted_iota(jnp.int32, (tq, tq), 1)
    causal = key_i <= qry_i
    g = g_ref[...]

    def reduce_keys(op, x):
        n = x.shape[0] // (SUBLANES * ATTN_REDUCE_WAYS)
        x = op(x.reshape(ATTN_REDUCE_WAYS, n, SUBLANES, x.shape[1]), axis=1)
        return op(op(x, axis=0), axis=0, keepdims=True)

    def scores(i, kj, q_parts):
        start = kj * tq if isinstance(kj, int) else pl.multiple_of(kj * tq, tq)
        hh = streams[i][0]
        kt = kb[pl.ds(start, tq), hh * vd:(hh + 1) * vd]
        return lax.dot_general(kt, q_parts[i], (((1,), (1,)), ((), ())),
                               preferred_element_type=F32)

    def consume(i, kj, state, s, masked):
        m, l = state
        if masked:
            s = jnp.where(causal, s, -jnp.inf)
        m_new = jnp.maximum(m, reduce_keys(jnp.max, s))
        p = jnp.exp2(s - m_new)
        alpha = jnp.exp2(m - m_new)
        l = alpha * l + reduce_keys(jnp.sum, p)
        acc[i] = alpha * acc[i] + jnp.dot(vt[streams[i][0], kj], p.astype(BF16),
                                          preferred_element_type=F32)
        return m_new, l

    def query_parts(qi):
        qstart = qi * tq if isinstance(qi, int) else pl.multiple_of(qi * tq, tq)
        parts = []
        for hh, c in streams:
            q = q_ref[pl.ds(qstart, tq), hh * vd:(hh + 1) * vd]
            zero = jnp.zeros_like(q)
            parts.append(jnp.where(first, zero, q) if c else jnp.where(first, q, zero))
        return parts

    def fill_scores(kj, q_parts):
        for i in range(len(streams)):
            sbuf[i] = scores(i, kj, q_parts)

    def q_block(qi, _):
        qstart = pl.multiple_of(qi * tq, tq)
        q_parts = query_parts(qi)
        acc[...] = jnp.zeros(acc.shape, F32)

        def kv_step(kj, carry):
            cur = [sbuf[i] for i in range(len(streams))]
            fill_scores(kj + 1, q_parts)
            return tuple(consume(i, kj, carry[i], cur[i], False) for i in range(len(streams)))

        init = tuple((jnp.full((1, tq), -jnp.inf, F32), jnp.zeros((1, tq), F32)) for _ in streams)
        carry = lax.fori_loop(0, qi, kv_step, init)
        diag = [sbuf[i] for i in range(len(streams))]
        fill_scores(0, query_parts(jnp.minimum(qi + 1, nq - 1)))
        fin = [consume(i, qi, carry[i], diag[i], True) for i in range(len(streams))]
        for hh in range(n_heads):
            l0, l1 = fin[2 * hh][1], fin[2 * hh + 1][1]
            o = (acc[2 * hh] / l0 - lam * (acc[2 * hh + 1] / l1)).T
            o = _rms_rows(o, g, RMS_EPS) * (1.0 - lam_init)
            o_ref[pl.ds(qstart, tq), hh * vd:(hh + 1) * vd] = o.astype(o_ref.dtype)
        return 0

    fill_scores(0, query_parts(0))
    lax.fori_loop(0, nq, q_block, 0)


def _attn_prompt(q, k, v, lams, g, batch, seq, n_heads, head_dim, lam_init, tq, heads_per_step):
    M, width = q.shape
    vd = width // n_heads
    hs = heads_per_step
    blk = pl.BlockSpec((seq, hs * vd), lambda b, h: (b, h))
    small = pl.BlockSpec((1, head_dim), lambda b, h: (0, 0))
    return pl.pallas_call(
        functools.partial(_attn_prompt_kernel, tq=tq, head_dim=head_dim, lam_init=lam_init),
        name="attn_prompt",
        grid=(batch, n_heads // hs),
        in_specs=[small, small, small, small,
                  pl.BlockSpec((1, vd), lambda b, h: (0, 0)), blk, blk, blk],
        out_specs=blk,
        out_shape=jax.ShapeDtypeStruct((M, width), BF16),
        scratch_shapes=[pltpu.VMEM((seq, hs * vd), BF16),
                        pltpu.VMEM((hs, seq // tq, vd, tq), BF16),
                        pltpu.VMEM((2 * hs, vd, tq), F32),
                        pltpu.VMEM((2 * hs, tq, tq), F32)],
        compiler_params=_params("parallel", "parallel"),
    )(*[x.reshape(1, head_dim) for x in lams], g.reshape(1, vd), q, k, v)


def _decode_attn_kernel(pt_ref, lq1, lk1, lq2, lk2, g_ref, q_ref, kn_ref, vn_ref, *refs,
                        n_pages, n_heads, head_dim, lam_init):
    del pt_ref
    k_refs = refs[:n_pages]
    v_refs = refs[n_pages:2 * n_pages]
    o_ref = refs[2 * n_pages]
    m_scr, l_scr, acc_scr = refs[2 * n_pages + 1:]
    c = pl.program_id(1)
    nc = pl.num_programs(1)
    rows_per_page, vd = k_refs[0].shape
    n_rows = 2 * n_heads

    @pl.when(c == 0)
    def _():
        m_scr[...] = jnp.full(m_scr.shape, -jnp.inf, F32)
        l_scr[...] = jnp.zeros(l_scr.shape, F32)
        acc_scr[...] = jnp.zeros(acc_scr.shape, F32)

    q = q_ref[...]
    lane = lax.broadcasted_iota(jnp.int32, q.shape, 1)
    q_rows = jnp.concatenate([jnp.where(lane < head_dim, q, 0.0),
                              jnp.where(lane < head_dim, 0.0, q)], axis=0).astype(BF16)
    assert n_heads & (n_heads - 1) == 0
    row_head = lax.broadcasted_iota(jnp.int32, (n_rows, rows_per_page), 0) & (n_heads - 1)
    col_head = lax.broadcasted_iota(jnp.int32, (n_rows, rows_per_page), 1) & (n_heads - 1)
    same_head = row_head == col_head

    scores = []
    for gi in range(n_pages):
        s = lax.dot_general(q_rows, k_refs[gi][...].astype(BF16), (((1,), (1,)), ((), ())),
                            preferred_element_type=F32)
        scores.append(jnp.where(same_head, s, -jnp.inf))
    m_old = m_scr[...]
    m_new = m_old
    for s in scores:
        m_new = jnp.maximum(m_new, jnp.max(s, axis=-1, keepdims=True))
    alpha = jnp.exp2(m_old - m_new)
    l_new = alpha * l_scr[...]
    acc = alpha * acc_scr[...]
    for gi in range(n_pages):
        p = jnp.exp2(scores[gi] - m_new)
        l_new = l_new + jnp.sum(p, axis=-1, keepdims=True)
        acc = acc + jnp.dot(p.astype(BF16), v_refs[gi][...].astype(BF16),
                            preferred_element_type=F32)
    m_scr[...] = m_new
    l_scr[...] = l_new
    acc_scr[...] = acc

    @pl.when(c == nc - 1)
    def _():
        lam = _diff_lambda(lq1, lk1, lq2, lk2, lam_init)
        kn2 = jnp.concatenate([kn_ref[...]] * 2, axis=0)
        vn2 = jnp.concatenate([vn_ref[...]] * 2, axis=0)
        s_new = jnp.sum(kn2 * q_rows.astype(F32), axis=-1, keepdims=True)
        m_fin = jnp.maximum(m_new, s_new)
        a_fin = jnp.exp2(m_new - m_fin)
        p_new = jnp.exp2(s_new - m_fin)
        l_fin = a_fin * l_new + p_new
        o2 = (a_fin * acc + p_new * vn2) / l_fin
        o = o2[0:n_heads] - lam * o2[n_heads:n_rows]
        o_ref[...] = _rms_rows(o, g_ref[...], RMS_EPS) * (1.0 - lam_init)


def _decode_attn(q, k_new, v_new, cache_k, cache_v, j, page_table, lams, g,
                 n_heads, head_dim, lam_init, pages_per_step):
    B = q.shape[0]
    vd = 2 * head_dim
    n_pages = page_table.shape[1]
    n_layers, n_pool, page = cache_k.shape[:3]
    G = math.gcd(pages_per_step, n_pages)
    small = pl.BlockSpec((1, head_dim), lambda b, c, pt: (0, 0))
    per_b = pl.BlockSpec((None, n_heads, vd), lambda b, c, pt: (b, 0, 0))
    ck = cache_k.reshape(n_layers, n_pool, page * n_heads, vd)
    cv = cache_v.reshape(n_layers, n_pool, page * n_heads, vd)

    def page_spec(gi):
        return pl.BlockSpec((None, None, page * n_heads, vd),
                            lambda b, c, pt: (j, pt[b, c * G + gi], 0, 0))

    grid_spec = pltpu.PrefetchScalarGridSpec(
        num_scalar_prefetch=1,
        grid=(B, n_pages // G),
        in_specs=[small, small, small, small,
                  pl.BlockSpec((1, vd), lambda b, c, pt: (0, 0)),
                  per_b, per_b, per_b]
                 + [page_spec(gi) for gi in range(G)] * 2,
        out_specs=per_b,
        scratch_shapes=[pltpu.VMEM((2 * n_heads, 1), F32), pltpu.VMEM((2 * n_heads, 1), F32),
                        pltpu.VMEM((2 * n_heads, vd), F32)],
    )
    return pl.pallas_call(
        functools.partial(_decode_attn_kernel, n_pages=G, n_heads=n_heads, head_dim=head_dim,
                          lam_init=lam_init),
        name="attn_decode",
        grid_spec=grid_spec,
        out_shape=jax.ShapeDtypeStruct((B, n_heads, vd), F32),
        compiler_params=_params("parallel", "arbitrary"),
    )(page_table, *[x.reshape(1, head_dim) for x in lams], g.reshape(1, vd),
      q.reshape(B, n_heads, vd), k_new.reshape(B, n_heads, vd), v_new.reshape(B, n_heads, vd),
      *([ck] * G), *([cv] * G))


def _mm_res_kernel(*refs, has_bias, emit_x):
    a_ref, w_ref = refs[:2]
    i = 2
    b_ref = None
    if has_bias:
        b_ref = refs[i]
        i += 1
    res_ref, g_ref = refs[i:i + 2]
    outs = refs[i + 2:]
    y = jnp.dot(a_ref[...].astype(BF16), w_ref[...], preferred_element_type=F32)
    if has_bias:
        y = y + b_ref[...]
    x = res_ref[...] + y
    if emit_x:
        outs[0][...] = x
    outs[-1][...] = _rms_rows(x, g_ref[...], RMS_EPS).astype(outs[-1].dtype)


def _mm_res(a, w, bias, res, g, tm, h_dtype, emit_x=True):
    M, K = a.shape
    N = w.shape[1]
    row = lambda m: (m, 0)
    fixed = lambda m: (0, 0)
    ins = [a, w]
    in_specs = [pl.BlockSpec((tm, K), row), pl.BlockSpec((K, N), fixed)]
    if bias is not None:
        ins.append(bias.reshape(1, N))
        in_specs.append(pl.BlockSpec((1, N), fixed))
    ins += [res, g.reshape(1, N)]
    in_specs += [pl.BlockSpec((tm, N), row), pl.BlockSpec((1, N), fixed)]
    out_specs = [pl.BlockSpec((tm, N), row)]
    out_shape = [jax.ShapeDtypeStruct((M, N), h_dtype)]
    if emit_x:
        out_specs = [pl.BlockSpec((tm, N), row)] + out_specs
        out_shape = [jax.ShapeDtypeStruct((M, N), F32)] + out_shape
    out = pl.pallas_call(
        functools.partial(_mm_res_kernel, has_bias=bias is not None, emit_x=emit_x),
        name="mm_res",
        grid=(M // tm,),
        in_specs=in_specs, out_specs=out_specs, out_shape=out_shape,
        compiler_params=_params("parallel"),
    )(*ins)
    return out if emit_x else (None, out[0])


def _glu_kernel(a_ref, w_ref, b_ref, u_ref, tail_ref, *, ch):
    a = a_ref[...]
    tm = a.shape[0]
    tail = tail_ref.shape[1]
    for c0 in range(0, ch, FFN_SLICE):
        c1 = c0 + FFN_SLICE
        lin = jnp.dot(a, w_ref[:, c0:c1], preferred_element_type=F32) + b_ref[:, c0:c1]
        gate = (jnp.dot(a, w_ref[:, ch + c0:ch + c1], preferred_element_type=F32)
                + b_ref[:, ch + c0:ch + c1])
        u = lin * jax.nn.sigmoid(gate)
        u_ref[:, c0:c1] = u
        tail_ref[0, :, c0:c1] = u[tm - tail:, :]


def _glu(a, w, b, tm, tail):
    M, K = a.shape
    ch = w.shape[1] // 2
    return pl.pallas_call(
        functools.partial(_glu_kernel, ch=ch),
        name="glu",
        grid=(M // tm,),
        in_specs=[pl.BlockSpec((tm, K), lambda m: (m, 0)),
                  pl.BlockSpec(w.shape, lambda m: (0, 0)),
                  pl.BlockSpec((1, 2 * ch), lambda m: (0, 0))],
        out_specs=[pl.BlockSpec((tm, ch), lambda m: (m, 0)),
                   pl.BlockSpec((1, tail, ch), lambda m: (m, 0, 0))],
        out_shape=[jax.ShapeDtypeStruct((M, ch), F32),
                   jax.ShapeDtypeStruct((M // tm, tail, ch), F32)],
        compiler_params=_params("parallel"),
    )(a, w, b.reshape(1, 2 * ch))


def _conv_prompt_kernel(u_ref, halo_ref, wdw_ref, bdw_ref, lng_ref, lnb_ref, w_ref, b_ref,
                        res_ref, g_ref, x_ref, h_ref, ubuf, cbuf, wb, *, tiles_per_seq, halo, chunk):
    m = pl.program_id(0)
    tm, ch = u_ref.shape
    width = wdw_ref.shape[0]
    seq_start = (m % tiles_per_seq) == 0
    ubuf[0, 0:halo, :] = jnp.where(seq_start, 0.0, halo_ref[...])
    ubuf[0, halo:halo + tm, :] = u_ref[...]
    n_rows = halo + tm
    base = ubuf[0]
    for r in range(1, SUBLANES):
        ubuf[r] = pltpu.roll(base, n_rows - r, axis=0)
    off = halo - (width - 1)
    wb[...] = jnp.broadcast_to(wdw_ref[...][:, None, :], wb.shape)
    bdw = bdw_ref[...]
    groups = chunk // SUBLANES
    for c0 in range(0, tm, chunk):
        acc = jnp.broadcast_to(bdw, (groups, SUBLANES, ch))
        for j in range(width):
            a, r = divmod(off + j, SUBLANES)
            win = ubuf[r, c0 + a * SUBLANES:c0 + a * SUBLANES + chunk, :]
            acc = acc + wb[j][None] * win.reshape(groups, SUBLANES, ch)
        cbuf[c0:c0 + chunk, :] = acc.reshape(chunk, ch)
    c = cbuf[...]
    mu = jnp.mean(c, axis=-1, keepdims=True)
    xc = c - mu
    y = xc * lax.rsqrt(jnp.mean(xc * xc, axis=-1, keepdims=True) + LN_EPS)
    y = y * lng_ref[...] + lnb_ref[...]
    y = y * jax.nn.sigmoid(y)
    out = jnp.dot(y.astype(BF16), w_ref[...], preferred_element_type=F32) + b_ref[...]
    x = res_ref[...] + out
    x_ref[...] = x
    h_ref[...] = _rms_rows(x, g_ref[...], RMS_EPS).astype(h_ref.dtype)


def _conv_prompt(u, w_dw, b_dw, ln_g, ln_b, w2, b2, res, g, seq, tm):
    M, ch = u.shape
    width = w_dw.shape[0]
    halo = -(-(width - 1) // SUBLANES) * SUBLANES
    row = lambda m: (m, 0)
    fixed = lambda m: (0, 0)
    vec = pl.BlockSpec((1, ch), fixed)
    return pl.pallas_call(
        functools.partial(_conv_prompt_kernel, tiles_per_seq=seq // tm, halo=halo,
                          chunk=CONV_CHUNK),
        name="conv_prompt",
        grid=(M // tm,),
        in_specs=[pl.BlockSpec((tm, ch), row),
                  pl.BlockSpec((halo, ch), lambda m: (jnp.maximum(m * (tm // halo) - 1, 0), 0)),
                  pl.BlockSpec((width, ch), fixed), vec, vec, vec,
                  pl.BlockSpec(w2.shape, fixed), vec,
                  pl.BlockSpec((tm, ch), row), vec],
        out_specs=[pl.BlockSpec((tm, ch), row)] * 2,
        out_shape=[jax.ShapeDtypeStruct((M, ch), F32), jax.ShapeDtypeStruct((M, ch), BF16)],
        scratch_shapes=[pltpu.VMEM((SUBLANES, halo + tm, ch), F32), pltpu.VMEM((tm, ch), F32),
                        pltpu.VMEM((width, SUBLANES, ch), F32)],
        compiler_params=_params("parallel"),
    )(u, u, w_dw, b_dw.reshape(1, ch), ln_g.reshape(1, ch), ln_b.reshape(1, ch), w2,
      b2.reshape(1, ch), res, g.reshape(1, ch))


def _ffn_prompt_kernel(h_ref, wu_ref, wdw_ref, bdw_ref, wd_ref, res_ref, g_ref,
                       *refs, tiles_per_seq, emit_x, sub):
    if emit_x:
        x_ref, hn_ref, st_ref, carry, gbuf = refs
    else:
        hn_ref, st_ref, carry, gbuf = refs
    m = pl.program_id(0)
    tm = h_ref.shape[0]
    d_ff = wd_ref.shape[0]
    pad = SUBLANES
    h = h_ref[...]
    seq_start = (m % tiles_per_seq) == 0
    for c0 in range(0, d_ff, sub):
        conv = []
        for half in range(2):
            cols = slice(half * d_ff + c0, half * d_ff + c0 + sub)
            u = jnp.dot(h, wu_ref[:, cols], preferred_element_type=F32)
            prev = jnp.where(seq_start, 0.0, carry[:, cols])
            carry[:, cols] = u[tm - pad:, :]
            st_ref[0, half, :, c0:c0 + sub] = u[tm - 2:, :]
            ext = jnp.concatenate([prev, u], axis=0)
            conv.append(wdw_ref[2:3, cols] * u + wdw_ref[1:2, cols] * ext[pad - 1:pad - 1 + tm]
                        + wdw_ref[0:1, cols] * ext[pad - 2:pad - 2 + tm] + bdw_ref[:, cols])
        gbuf[:, c0:c0 + sub] = ((conv[0] * jax.nn.sigmoid(conv[0])) * conv[1]).astype(BF16)
    y = jnp.dot(gbuf[...], wd_ref[...], preferred_element_type=F32)
    x = res_ref[...] + y
    if emit_x:
        x_ref[...] = x
    hn_ref[...] = _rms_rows(x, g_ref[...], RMS_EPS).astype(hn_ref.dtype)


def _ffn_prompt(h, w_up, w_dw, b_dw, w_down, res, g, batch, seq, tm, sub, h_dtype, emit_x):
    M, D = h.shape
    d_ff = w_down.shape[0]
    tiles_per_seq = seq // tm
    row = lambda m: (m, 0)
    fixed = lambda m: (0, 0)
    once = dict(pipeline_mode=pl.Buffered(1))
    out_specs = [pl.BlockSpec((tm, D), row),
                 pl.BlockSpec((1, 2, 2, d_ff), lambda m: (m, 0, 0, 0))]
    out_shape = [jax.ShapeDtypeStruct((M, D), h_dtype),
                 jax.ShapeDtypeStruct((M // tm, 2, 2, d_ff), F32)]
    if emit_x:
        out_specs = [pl.BlockSpec((tm, D), row)] + out_specs
        out_shape = [jax.ShapeDtypeStruct((M, D), F32)] + out_shape
    out = pl.pallas_call(
        functools.partial(_ffn_prompt_kernel, tiles_per_seq=tiles_per_seq, emit_x=emit_x, sub=sub),
        name="ffn_prompt",
        grid=(M // tm,),
        in_specs=[pl.BlockSpec((tm, D), row),
                  pl.BlockSpec((D, 2 * d_ff), fixed, **once),
                  pl.BlockSpec((3, 2 * d_ff), fixed, **once),
                  pl.BlockSpec((1, 2 * d_ff), fixed, **once),
                  pl.BlockSpec((d_ff, D), fixed, **once),
                  pl.BlockSpec((tm, D), row),
                  pl.BlockSpec((1, D), fixed, **once)],
        out_specs=out_specs, out_shape=out_shape,
        scratch_shapes=[pltpu.VMEM((SUBLANES, 2 * d_ff), F32), pltpu.VMEM((tm, d_ff), BF16)],
        compiler_params=_params("arbitrary"),
    )(h, w_up, w_dw, b_dw.reshape(1, 2 * d_ff), w_down, res, g.reshape(1, D))
    if emit_x:
        x, hn, st = out
    else:
        (hn, st), x = out, None
    st = st[tiles_per_seq - 1::tiles_per_seq].transpose(0, 2, 1, 3).reshape(batch, 2, 2 * d_ff)
    return x, hn, st


def _mm_plain_kernel(a_ref, w_ref, o_ref):
    o_ref[...] = jnp.dot(a_ref[...], w_ref[...], preferred_element_type=F32)


def _mm_plain(a, w, tn):
    M, K = a.shape
    N = w.shape[1]
    return pl.pallas_call(
        _mm_plain_kernel,
        name="mm_plain",
        grid=(N // tn,),
        in_specs=[pl.BlockSpec((M, K), lambda n: (0, 0)), pl.BlockSpec((K, tn), lambda n: (0, n))],
        out_specs=pl.BlockSpec((M, tn), lambda n: (0, n)),
        out_shape=jax.ShapeDtypeStruct((M, N), F32),
        compiler_params=_params("parallel"),
    )(a, w)


def _ffn_gate_sample_kernel(u_ref, s0_ref, s1_ref, w_ref, b_ref, o_ref, *, d_ff):
    cv = (w_ref[2:3, :] * u_ref[...] + w_ref[1:2, :] * s1_ref[...]
          + w_ref[0:1, :] * s0_ref[...] + b_ref[...])
    a = cv[:, 0:d_ff]
    o_ref[...] = ((a * jax.nn.sigmoid(a)) * cv[:, d_ff:2 * d_ff]).astype(o_ref.dtype)


def _ffn_gate_sample(u, state, w_dw, b_dw):
    B, two_ff = u.shape
    d_ff = two_ff // 2
    full = lambda s: pl.BlockSpec(s, lambda i: (0, 0))
    return pl.pallas_call(
        functools.partial(_ffn_gate_sample_kernel, d_ff=d_ff),
        name="ffn_gate_sample",
        grid=(1,),
        in_specs=[full((B, two_ff))] * 3 + [full((3, two_ff)), full((1, two_ff))],
        out_specs=full((B, d_ff)),
        out_shape=jax.ShapeDtypeStruct((B, d_ff), BF16),
        compiler_params=_params("arbitrary"),
    )(u, state[:, 0], state[:, 1], w_dw, b_dw.reshape(1, two_ff))


def _conv_sample_kernel(u_ref, st_ref, wdw_ref, bdw_ref, lng_ref, lnb_ref, o_ref):
    width = wdw_ref.shape[0]
    c = jnp.sum(st_ref[...] * wdw_ref[0:width - 1, :][None], axis=1)
    c = c + u_ref[...] * wdw_ref[width - 1:width, :] + bdw_ref[...]
    mu = jnp.mean(c, axis=-1, keepdims=True)
    xc = c - mu
    y = xc * lax.rsqrt(jnp.mean(xc * xc, axis=-1, keepdims=True) + LN_EPS)
    y = y * lng_ref[...] + lnb_ref[...]
    o_ref[...] = (y * jax.nn.sigmoid(y)).astype(o_ref.dtype)


def _conv_sample(u, state, w_dw, b_dw, ln_g, ln_b):
    B, ch = u.shape
    full2 = lambda s: pl.BlockSpec(s, lambda i: (0, 0))
    vec = full2((1, ch))
    return pl.pallas_call(
        _conv_sample_kernel,
        name="conv_sample",
        grid=(1,),
        in_specs=[full2((B, ch)), pl.BlockSpec(state.shape, lambda i: (0, 0, 0)),
                  full2(w_dw.shape), vec, vec, vec],
        out_specs=full2((B, ch)),
        out_shape=jax.ShapeDtypeStruct((B, ch), BF16),
        compiler_params=_params("arbitrary"),
    )(u, state, w_dw, b_dw.reshape(1, ch), ln_g.reshape(1, ch), ln_b.reshape(1, ch))


def kernel(x_prompt, x_sample, cache_k, cache_v, page_table, state_conv, state_ffn, norm_mix, norm_ffn, norm_final, w_qkv, w_o, lambda_q1, lambda_k1, lambda_q2, lambda_k2, subln_g, w_pw1, b_pw1, w_dw, b_dw, ln_g, ln_b, w_pw2, b_pw2, w_up, w_ffn_dw, b_ffn_dw, w_down):
    Bp, T, D = x_prompt.shape
    Bs, Ts, _ = x_sample.shape
    assert Ts == 1, "sample group is one token per sequence"
    depth = norm_mix.shape[0]
    head_dim = lambda_q1.shape[1]
    vd = subln_g.shape[1]
    n_heads = w_o.shape[1] // vd
    scale = head_dim ** -0.5 * math.log2(math.e)
    Mp, Ms = Bp * T, Bs * Ts

    xp = x_prompt.reshape(Mp, D)
    xs = x_sample.reshape(Ms, D)
    hp = hs = None

    wanted = []
    for i in range(depth):
        j = i // N_MIXERS
        wanted += [(w_qkv, j), (w_o, j)] if i % N_MIXERS == 0 else [(w_pw1, j), (w_pw2, j)]
        wanted += [(w_up, i), (w_down, i)]
    bf16_of = dict(zip([(id(w), i) for w, i in wanted], _layers_bf16(wanted)))

    kp_l, vp_l, ks_l, vs_l = [], [], [], []
    cp_l, cs_l, fp_l, fs_l = [], [], [], []
    for i in range(depth):
        j = i // N_MIXERS
        if i % N_MIXERS == 0:
            lam_init = 0.8 - 0.6 * math.exp(-0.3 * i)
            lams = (lambda_q1[j], lambda_k1[j], lambda_q2[j], lambda_k2[j])
            wq = bf16_of[id(w_qkv), j]
            wo = bf16_of[id(w_o), j]
            qp, kp, vp = _qkv(xp, norm_mix[i], wq, ROW_TILE, scale, BF16)
            qs, ks, vs = _qkv(xs, norm_mix[i], wq, Ms, scale, F32)
            op = _attn_prompt(qp, kp, vp, lams, subln_g[j], Bp, T, n_heads, head_dim, lam_init,
                              ATTN_BLOCK, ATTN_HEADS_PER_STEP)
            os_ = _decode_attn(qs, ks, vs, cache_k, cache_v, j, page_table, lams, subln_g[j],
                               n_heads, head_dim, lam_init, DECODE_PAGES_PER_STEP)
            xp, hp = _mm_res(op, wo, None, xp, norm_ffn[i], WIDE_ROW_TILE, BF16)
            xs, hs = _mm_res(os_.reshape(Ms, n_heads * vd), wo, None, xs, norm_ffn[i], Ms, BF16)
            kp_l.append(kp.reshape(Bp, T, n_heads, vd))
            vp_l.append(vp.reshape(Bp, T, n_heads, vd))
            ks_l.append(ks.reshape(Bs, Ts, n_heads, vd))
            vs_l.append(vs.reshape(Bs, Ts, n_heads, vd))
        else:
            w1 = bf16_of[id(w_pw1), j]
            w2 = bf16_of[id(w_pw2), j]
            width = w_dw.shape[1]
            tail = -(-(width - 1) // SUBLANES) * SUBLANES
            up, up_tails = _glu(hp, w1, b_pw1[j], ROW_TILE, tail)
            us, _ = _glu(hs, w1, b_pw1[j], Ms, min(SUBLANES, Ms))
            cs = _conv_sample(us, state_conv[j], w_dw[j], b_dw[j], ln_g[j], ln_b[j])
            xp, hp = _conv_prompt(up, w_dw[j], b_dw[j], ln_g[j], ln_b[j], w2, b_pw2[j], xp,
                                  norm_ffn[i], T, ROW_TILE)
            xs, hs = _mm_res(cs, w2, b_pw2[j], xs, norm_ffn[i], Ms, BF16)
            per_seq = T // ROW_TILE
            cp_l.append(up_tails[per_seq - 1::per_seq][:, tail - (width - 1):])
            cs_l.append(jnp.concatenate([state_conv[j][:, 1:], us[:, None]], axis=1))
        last = i == depth - 1
        g_next = norm_final if last else norm_mix[i + 1]
        h_dtype = F32 if last else BF16
        wu = bf16_of[id(w_up), i]
        wd = bf16_of[id(w_down), i]
        d_ff = wd.shape[0]
        xp, hp, fbp = _ffn_prompt(hp, wu, w_ffn_dw[i], b_ffn_dw[i], wd, xp, g_next, Bp, T,
                                  WIDE_ROW_TILE, FFN_SLICE, h_dtype, not last)
        u_s = _mm_plain(hs, wu, d_ff // 2)
        gs = _ffn_gate_sample(u_s, state_ffn[i], w_ffn_dw[i], b_ffn_dw[i])
        xs, hs = _mm_res(gs, wd, None, xs, g_next, Ms, h_dtype, emit_x=not last)
        fp_l.append(fbp)
        fs_l.append(jnp.concatenate([state_ffn[i][:, 1:], u_s[:, None]], axis=1))

    y_prompt = hp.reshape(Bp, T, D)
    y_sample = hs.reshape(Bs, Ts, D)
    return (y_prompt, y_sample, jnp.stack(kp_l), jnp.stack(vp_l), jnp.stack(ks_l), jnp.stack(vs_l),
            jnp.stack(cp_l), jnp.stack(cs_l), jnp.stack(fp_l), jnp.stack(fs_l))
```
